```python
import math
import jax, jax.numpy as jnp
from jax import lax
import numpy as np

D_MODEL = 1024
BATCH = 8
SEQ = 4096
DEPTH = 4

MIX_WIDTH = D_MODEL
A_HEADS = 8
A_KV_HEADS = 2
A_GROUP = A_HEADS // A_KV_HEADS
B_HEADS = 8
HEAD_DIM = MIX_WIDTH // (A_HEADS + B_HEADS)
A_WIDTH = A_HEADS * HEAD_DIM
A_KV_WIDTH = A_KV_HEADS * HEAD_DIM
B_WIDTH = B_HEADS * HEAD_DIM
PROJ_WIDTH = A_WIDTH + 2 * A_KV_WIDTH + 3 * B_WIDTH
WINDOW = 128
BLOCK = 128
GRID_W = 64
NA_ROWS_MAX = 8
NA_COLS = 16
T5_BUCKETS = 32
T5_MAX_DIST = 128
PEER_HEADS = 8
PEER_N_KEYS = 128
PEER_N_EXPERTS = PEER_N_KEYS * PEER_N_KEYS
PEER_TOPK = 16
PEER_QDIM = 256
PEER_CHUNK = 128
ALPHA = (2 * DEPTH) ** 0.25
BETA = (8 * DEPTH) ** -0.25
LN_EPS = 1e-5
NEG = -1e30

kernel_name = 'hymba_window_natten_peer_deepnorm'


def layer_norm(x, g, b):
    xf = x.astype(jnp.float32)
    mu = xf.mean(-1, keepdims=True)
    var = jnp.square(xf - mu).mean(-1, keepdims=True)
    return ((xf - mu) * lax.rsqrt(var + LN_EPS) * g + b).astype(x.dtype)


def rms_norm(x, g):
    xf = x.astype(jnp.float32)
    return (xf * lax.rsqrt(jnp.square(xf).mean(-1, keepdims=True) + LN_EPS) * g).astype(x.dtype)


def t5_bucket(rel):
    nb = T5_BUCKETS // 2
    max_exact = nb // 2
    ret = (rel > 0).astype(jnp.int32) * nb
    n = jnp.abs(rel).astype(jnp.int32)
    nf = jnp.maximum(n, 1).astype(jnp.float32)
    large = max_exact + (jnp.log(nf / max_exact) / math.log(T5_MAX_DIST / max_exact)
                         * (nb - max_exact)).astype(jnp.int32)
    large = jnp.minimum(large, nb - 1)
    return ret + jnp.where(n < max_exact, n, large)


def window_attention(q, k, v, sink, t5_table):
    b, t = q.shape[0], q.shape[1]
    nb = t // BLOCK
    pad = ((0, 0), (BLOCK, BLOCK), (0, 0), (0, 0))
    kp = jnp.pad(k, pad).reshape(b, nb + 2, BLOCK, A_KV_HEADS, HEAD_DIM)
    vp = jnp.pad(v, pad).reshape(b, nb + 2, BLOCK, A_KV_HEADS, HEAD_DIM)
    kw = jnp.concatenate([kp[:, :-2], kp[:, 1:-1], kp[:, 2:]], axis=2)
    vw = jnp.concatenate([vp[:, :-2], vp[:, 1:-1], vp[:, 2:]], axis=2)
    qb = q.reshape(b, nb, BLOCK, A_KV_HEADS, A_GROUP, HEAD_DIM)
    s = jnp.einsum('bnqkgd,bnskd->bnkgqs', qb, kw).astype(jnp.float32) * (HEAD_DIM ** -0.5)
    rel = (jnp.arange(3 * BLOCK)[None, :] - BLOCK) - jnp.arange(BLOCK)[:, None]
    bias = t5_table[t5_bucket(rel)].astype(jnp.float32)
    bias = bias.transpose(2, 0, 1).reshape(A_KV_HEADS, A_GROUP, BLOCK, 3 * BLOCK)
    key_pos = jnp.arange(nb)[:, None] * BLOCK + jnp.arange(3 * BLOCK)[None, :] - BLOCK
    in_range = (key_pos >= 0) & (key_pos < t)
    valid = (jnp.abs(rel) <= WINDOW)[None] & in_range[:, None, :]
    s = jnp.where(valid[None, :, None, None], s + bias, NEG)
    sk = sink.astype(jnp.float32).reshape(A_KV_HEADS, A_GROUP)[None, None, :, :, None, None]
    m = jnp.maximum(s.max(-1, keepdims=True), sk)
    e = jnp.exp(s - m)
    p = e / (e.sum(-1, keepdims=True) + jnp.exp(sk - m))
    o = jnp.einsum('bnkgqs,bnskd->bnqkgd', p.astype(v.dtype), vw)
    return o.reshape(b, t, A_WIDTH)


def neighbourhood_attention(q, k, v, rpb):
    b, t = q.shape[0], q.shape[1]
    rows = t // GRID_W
    kh = min(NA_ROWS_MAX, rows)
    qg = q.reshape(b, rows, GRID_W, B_HEADS, HEAD_DIM)
    kg = k.reshape(b, rows, GRID_W, B_HEADS, HEAD_DIM)
    vg = v.reshape(b, rows, GRID_W, B_HEADS, HEAD_DIM)
    r = jnp.arange(rows)
    row_start = jnp.clip(r - kh // 2, 0, rows - kh)
    key_rows = row_start[:, None] + jnp.arange(kh)[None, :]
    kr = kg[:, key_rows]
    vr = vg[:, key_rows]
    s = jnp.einsum('brqhd,brikhd->brhqik', qg, kr).astype(jnp.float32) * (HEAD_DIM ** -0.5)
    c = jnp.arange(GRID_W)
    col_start = jnp.clip(c - NA_COLS // 2, 0, GRID_W - NA_COLS)
    col_mask = (c[None, :] >= col_start[:, None]) & (c[None, :] < col_start[:, None] + NA_COLS)
    dr = key_rows - r[:, None] + (NA_ROWS_MAX - 1)
    dc = jnp.clip(c[None, :] - c[:, None], -(NA_COLS - 1), NA_COLS - 1) + (NA_COLS - 1)
    bias = rpb[:, dr[:, None, :, None], dc[None, :, None, :]].astype(jnp.float32)
    bias = bias.transpose(1, 0, 2, 3, 4)
    s = jnp.where(col_mask[:, None, :], s + bias[None], NEG)
    p = jax.nn.softmax(s.reshape(b, rows, B_HEADS, GRID_W, kh * GRID_W), axis=-1)
    p = p.reshape(b, rows, B_HEADS, GRID_W, kh, GRID_W).astype(v.dtype)
    o = jnp.einsum('brhqik,brikhd->brqhd', p, vr)
    return o.reshape(b, t, B_WIDTH)


def peer(x, w_q, sub_keys, u, v):
    b, t, d = x.shape
    n = b * t
    xf = x.reshape(n, d)
    q = (xf @ w_q).reshape(n, PEER_HEADS, 2, PEER_QDIM // 2)
    sc = jnp.einsum('nhpc,hpkc->nhpk', q, sub_keys).astype(jnp.float32)
    s_top, i_top = lax.top_k(sc, PEER_TOPK)
    cand = (s_top[:, :, 0, :, None] + s_top[:, :, 1, None, :]).reshape(n, PEER_HEADS, PEER_TOPK * PEER_TOPK)
    cand_idx = (i_top[:, :, 0, :, None] * PEER_N_KEYS + i_top[:, :, 1, None, :]).reshape(
        n, PEER_HEADS, PEER_TOPK * PEER_TOPK)
    fs, fpos = lax.top_k(cand, PEER_TOPK)
    eidx = jnp.take_along_axis(cand_idx, fpos, axis=-1)
    g = jax.nn.softmax(fs, axis=-1)
    nc = n // PEER_CHUNK

    def chunk_fn(args):
        xc, ec, gc = args
        h = jnp.einsum('cd,chkd->chk', xc, u[ec])
        a = (jax.nn.gelu(h.astype(jnp.float32), approximate=False) * gc).astype(xc.dtype)
        return jnp.einsum('chk,chkd->cd', a, v[ec])

    out = lax.map(chunk_fn, (xf.reshape(nc, PEER_CHUNK, d),
                             eidx.reshape(nc, PEER_CHUNK, PEER_HEADS, PEER_TOPK),
                             g.reshape(nc, PEER_CHUNK, PEER_HEADS, PEER_TOPK)))
    return out.reshape(b, t, d)


def setup_inputs(seed: int = 0) -> dict:
    key = jax.random.key(seed)
    ks = jax.random.split(key, 16)
    f32 = jnp.float32
    x = jax.random.normal(ks[0], (BATCH, SEQ, D_MODEL), f32)
    va0 = A_WIDTH + A_KV_WIDTH
    vb0 = A_WIDTH + 2 * A_KV_WIDTH + 2 * B_WIDTH
    col_scale = jnp.ones((PROJ_WIDTH,), f32).at[va0:va0 + A_KV_WIDTH].set(BETA).at[vb0:].set(BETA)
    w_in = jax.random.normal(ks[1], (DEPTH, D_MODEL, PROJ_WIDTH), f32) * (D_MODEL ** -0.5) * col_scale
    w_o = jax.random.normal(ks[2], (DEPTH, MIX_WIDTH, D_MODEL), f32) * (MIX_WIDTH ** -0.5) * BETA
    attn_sink = jax.random.normal(ks[3], (DEPTH, A_HEADS), f32) * 0.5
    na_rpb = jax.random.normal(ks[4], (DEPTH, B_HEADS, 2 * NA_ROWS_MAX - 1, 2 * NA_COLS - 1), f32) * 0.1
    t5_table = jax.random.normal(ks[5], (T5_BUCKETS, A_HEADS), f32) * 0.1
    gnorm_a = 1.0 + 0.02 * jax.random.normal(ks[6], (DEPTH, A_WIDTH), f32)
    gnorm_b = 1.0 + 0.02 * jax.random.normal(ks[7], (DEPTH, B_WIDTH), f32)
    ln1_g = 1.0 + 0.02 * jax.random.normal(ks[8], (DEPTH, D_MODEL), f32)
    ln1_b = 0.02 * jax.random.normal(ks[9], (DEPTH, D_MODEL), f32)
    ln2_g = 1.0 + 0.02 * jax.random.normal(ks[10], (DEPTH, D_MODEL), f32)
    ln2_b = 0.02 * jax.random.normal(ks[11], (DEPTH, D_MODEL), f32)
    peer_wq = jax.random.normal(ks[12], (DEPTH, D_MODEL, PEER_HEADS * PEER_QDIM), f32) * (D_MODEL ** -0.5)
    peer_keys = jax.random.normal(ks[13], (DEPTH, PEER_HEADS, 2, PEER_N_KEYS, PEER_QDIM // 2), f32) * (
        (PEER_QDIM // 2) ** -0.5)
    peer_u = jax.random.normal(ks[14], (DEPTH, PEER_N_EXPERTS, D_MODEL), f32) * (D_MODEL ** -0.5)
    peer_v = jax.random.normal(ks[15], (DEPTH, PEER_N_EXPERTS, D_MODEL), f32) * BETA * (PEER_HEADS ** -0.5)
    return {'x': x, 'w_in': w_in, 'w_o': w_o, 'attn_sink': attn_sink, 'na_rpb': na_rpb,
            't5_table': t5_table, 'gnorm_a': gnorm_a, 'gnorm_b': gnorm_b,
            'ln1_g': ln1_g, 'ln1_b': ln1_b, 'ln2_g': ln2_g, 'ln2_b': ln2_b,
            'peer_wq': peer_wq, 'peer_keys': peer_keys, 'peer_u': peer_u, 'peer_v': peer_v}


def reference(x, w_in, w_o, attn_sink, na_rpb, t5_table, gnorm_a, gnorm_b,
              ln1_g, ln1_b, ln2_g, ln2_b, peer_wq, peer_keys, peer_u, peer_v):
    b, t, _ = x.shape
    o1 = A_WIDTH
    o2 = o1 + A_KV_WIDTH
    o3 = o2 + A_KV_WIDTH
    o4 = o3 + B_WIDTH
    o5 = o4 + B_WIDTH
    for l in range(DEPTH):
        h = x @ w_in[l]
        qa = h[..., :o1].reshape(b, t, A_HEADS, HEAD_DIM)
        ka = h[..., o1:o2].reshape(b, t, A_KV_HEADS, HEAD_DIM)
        va = h[..., o2:o3].reshape(b, t, A_KV_HEADS, HEAD_DIM)
        qb = h[..., o3:o4].reshape(b, t, B_HEADS, HEAD_DIM)
        kb = h[..., o4:o5].reshape(b, t, B_HEADS, HEAD_DIM)
        vb = h[..., o5:].reshape(b, t, B_HEADS, HEAD_DIM)
        ya = window_attention(qa, ka, va, attn_sink[l], t5_table)
        yb = neighbourhood_attention(qb, kb, vb, na_rpb[l])
        mixed = jnp.concatenate([rms_norm(ya, gnorm_a[l]), rms_norm(yb, gnorm_b[l])], axis=-1) @ w_o[l]
        x = layer_norm(ALPHA * x + mixed, ln1_g[l], ln1_b[l])
        x = layer_norm(ALPHA * x + peer(x, peer_wq[l], peer_keys[l], peer_u[l], peer_v[l]), ln2_g[l], ln2_b[l])
    return x
```

```python
import functools
import math

import numpy as np
import jax
import jax.numpy as jnp
from jax import lax
from jax.experimental import pallas as pl
from jax.experimental.pallas import tpu as pltpu

F32 = jnp.float32
BF16 = jnp.bfloat16

HEAD_DIM = 64
A_HEADS = 8
A_KV_HEADS = 2
B_HEADS = 8
A_WIDTH = A_HEADS * HEAD_DIM
A_KV_WIDTH = A_KV_HEADS * HEAD_DIM
B_WIDTH = B_HEADS * HEAD_DIM
WINDOW = 128
BLOCK = 128
GRID_W = 64
NA_ROWS = 8
NA_COLS = 16
T5_BUCKETS = 32
T5_MAX_DIST = 128
PEER_HEADS = 8
PEER_N_KEYS = 128
PEER_TOPK = 16
PEER_QDIM = 256
LN_EPS = 1e-5
NEG = -1e30

LANES = 128
NA_GROUP_ROWS = 4
NA_KEY_ROWS = 12
Q_WIDTH = A_WIDTH + B_WIDTH
KV_WIDTH = 2 * A_KV_WIDTH + 2 * A_KV_WIDTH + B_WIDTH + 2 * B_WIDTH
KV_KA, KV_VA, KV_KB, KV_VB = 0, 2 * A_KV_WIDTH, 4 * A_KV_WIDTH, 4 * A_KV_WIDTH + B_WIDTH

VMEM_LIMIT = 56 * 1024 * 1024


def _dot(a, b):
    return jnp.dot(a, b, preferred_element_type=F32)


def _dot_nt(a, b):
    return lax.dot_general(a, b, (((1,), (1,)), ((), ())), preferred_element_type=F32)


def _layer_norm(y, g, b):
    mu = jnp.mean(y, axis=-1, keepdims=True)
    d = y - mu
    var = jnp.mean(d * d, axis=-1, keepdims=True)
    return d * lax.rsqrt(var + LN_EPS) * g + b


def _rms_norm(y, g):
    return y * lax.rsqrt(jnp.mean(y * y, axis=-1, keepdims=True) + LN_EPS) * g


def _lookup_kernel(tab_ref, idx_ref, mask_ref, o_ref):
    n_entries = tab_ref.shape[1]
    width = idx_ref.shape[1]
    onehot = (lax.broadcasted_iota(jnp.int32, (n_entries, width), 0) == idx_ref[...]).astype(F32)
    val = jnp.dot(tab_ref[...], onehot, preferred_element_type=F32, precision=lax.Precision.HIGHEST)
    o_ref[...] = jnp.where(mask_ref[...] > 0, val, NEG)


def _lookup(tab, idx, mask, chunk):
    rows, n_entries = tab.shape
    width = idx.shape[1]
    return pl.pallas_call(
        _lookup_kernel,
        grid=(width // chunk,),
        in_specs=[pl.BlockSpec((rows, n_entries), lambda i: (0, 0)),
                  pl.BlockSpec((1, chunk), lambda i: (0, i)),
                  pl.BlockSpec((1, chunk), lambda i: (0, i))],
        out_specs=pl.BlockSpec((rows, chunk), lambda i: (0, i)),
        out_shape=jax.ShapeDtypeStruct((rows, width), F32),
        name="bias_lookup",
    )(tab, idx, mask)


def _t5_bucket(rel):
    nb = T5_BUCKETS // 2
    max_exact = nb // 2
    ret = (rel > 0).astype(jnp.int32) * nb
    n = jnp.abs(rel).astype(jnp.int32)
    nf = jnp.maximum(n, 1).astype(jnp.float32)
    large = max_exact + (jnp.log(nf / max_exact) / math.log(T5_MAX_DIST / max_exact)
                         * (nb - max_exact)).astype(jnp.int32)
    large = jnp.minimum(large, nb - 1)
    return ret + jnp.where(n < max_exact, n, large)


def _window_bias_table(t5_table):
    q = jnp.arange(BLOCK)[:, None]
    s = jnp.arange(3 * BLOCK)[None, :]
    rel = jnp.stack([s - q, s - BLOCK - q, s - 2 * BLOCK - q])
    idx = _t5_bucket(rel).reshape(1, -1)
    mask = (jnp.abs(rel) <= WINDOW).astype(F32).reshape(1, -1)
    tab = _lookup(t5_table.T.astype(F32), idx, mask, 3 * BLOCK * 32)
    tab = tab.reshape(A_KV_HEADS, 2, 2, 3, BLOCK, 3 * BLOCK)
    tab = tab.transpose(3, 0, 2, 1, 4, 5)
    return tab.reshape(3, A_KV_HEADS, 2, 2 * BLOCK, 3 * BLOCK)


def _na_bias_table(na_rpb):
    depth = na_rpb.shape[0]
    n_dr, n_dc = 2 * NA_ROWS - 1, 2 * NA_COLS - 1
    c = jnp.arange(GRID_W)
    col_start = jnp.clip(c - NA_COLS // 2, 0, GRID_W - NA_COLS)
    col_mask = (c[None, :] >= col_start[:, None]) & (c[None, :] < col_start[:, None] + NA_COLS)
    dc = jnp.clip(c[None, :] - c[:, None], -(NA_COLS - 1), NA_COLS - 1) + (NA_COLS - 1)
    tab = jnp.pad(na_rpb.reshape(depth * B_HEADS * n_dr, n_dc).astype(F32), ((0, 0), (0, 1)))
    out = _lookup(tab, dc.reshape(1, -1).astype(jnp.int32), col_mask.astype(F32).reshape(1, -1),
                  GRID_W * GRID_W)
    out = out.reshape(depth, B_HEADS, n_dr, GRID_W, GRID_W)
    out = jnp.pad(out, ((0, 0), (0, 0), (1, 2), (0, 0), (0, 0)))
    return jnp.concatenate([out[:, :, :-1], out[:, :, 1:]], axis=-1)


def _proj_kernel(x_ref, w_ref, hq_ref, hkv_ref):
    h = _dot(x_ref[...].astype(BF16), w_ref[...])
    hq_ref[...] = (h[:, :Q_WIDTH] * (HEAD_DIM ** -0.5)).astype(BF16)
    hkv_ref[...] = h[:, Q_WIDTH:].astype(BF16)


def _proj(x2d, w_cat, tm):
    n, d = x2d.shape
    return pl.pallas_call(
        _proj_kernel,
        grid=(n // tm,),
        in_specs=[pl.BlockSpec((tm, d), lambda i: (i, 0)),
                  pl.BlockSpec(w_cat.shape, lambda i: (0, 0))],
        out_specs=[pl.BlockSpec((tm, Q_WIDTH), lambda i: (i, 0)),
                   pl.BlockSpec((tm, KV_WIDTH), lambda i: (i, 0))],
        out_shape=[jax.ShapeDtypeStruct((n, Q_WIDTH), BF16),
                   jax.ShapeDtypeStruct((n, KV_WIDTH), BF16)],
        compiler_params=pltpu.CompilerParams(dimension_semantics=("parallel",),
                                             vmem_limit_bytes=VMEM_LIMIT),
        name="proj",
    )(x2d, w_cat)


def _proj_weight(w_in_l):
    o1 = A_WIDTH
    o2 = o1 + A_KV_WIDTH
    o3 = o2 + A_KV_WIDTH
    o4 = o3 + B_WIDTH
    o5 = o4 + B_WIDTH

    def dup(w, heads):
        w = w.reshape(w.shape[0], heads, 1, HEAD_DIM)
        return jnp.broadcast_to(w, (w.shape[0], heads, 2, HEAD_DIM)).reshape(w.shape[0], heads * 2 * HEAD_DIM)

    cols = [w_in_l[:, :o1], w_in_l[:, o3:o4],
            dup(w_in_l[:, o1:o2], A_KV_HEADS), dup(w_in_l[:, o2:o3], A_KV_HEADS),
            w_in_l[:, o4:o5], dup(w_in_l[:, o5:], B_HEADS)]
    return jnp.concatenate(cols, axis=1).astype(BF16)


def _attn_kernel(sink_ref, hq_ref, hkv_ref, x_ref, wb_ref, cb_ref, wo_ref, ga_ref, gb_ref,
                 lg_ref, lb_ref, o_ref, ya_scr, yb_scr, *, seq, tq, alpha):
    t = pl.program_id(1)
    n_blocks = seq // BLOCK
    rows = seq // GRID_W
    low_half = lax.broadcasted_iota(jnp.int32, (1, LANES), 1) < HEAD_DIM

    def split_halves(kd):
        zero = jnp.zeros_like(kd)
        return jnp.where(low_half, kd, zero), jnp.where(low_half, zero, kd)

    top_rows = lax.broadcasted_iota(jnp.int32, (2 * BLOCK, 1), 0) < BLOCK
    for j in range(tq // BLOCK):
        nb = t * (tq // BLOCK) + j
        kstart = pl.multiple_of(jnp.clip(nb - 1, 0, n_blocks - 3) * BLOCK, BLOCK)
        var = jnp.where(nb == 0, 0, jnp.where(nb == n_blocks - 1, 2, 1))
        qrows = slice(j * BLOCK, (j + 1) * BLOCK)
        for kv in range(A_KV_HEADS):
            c0 = kv * 2 * LANES
            q2 = jnp.concatenate([hq_ref[0, qrows, c0:c0 + LANES],
                                  hq_ref[0, qrows, c0 + LANES:c0 + 2 * LANES]], axis=0)
            kd = hkv_ref[0, pl.ds(kstart, 3 * BLOCK), KV_KA + kv * LANES:KV_KA + (kv + 1) * LANES]
            vd = hkv_ref[0, pl.ds(kstart, 3 * BLOCK), KV_VA + kv * LANES:KV_VA + (kv + 1) * LANES]
            outs = []
            for par, kk in enumerate(split_halves(kd)):
                s = _dot_nt(q2, kk) + wb_ref[var, kv, par]
                sk = jnp.where(top_rows, sink_ref[4 * kv + par], sink_ref[4 * kv + par + 2])
                m = jnp.maximum(jnp.max(s, axis=-1, keepdims=True), sk)
                e = jnp.exp(s - m)
                den = jnp.sum(e, axis=-1, keepdims=True) + jnp.exp(sk - m)
                p = e * (1.0 / den)
                outs.append(_dot(p.astype(BF16), vd))
            slab = jnp.where(low_half, outs[0], outs[1])
            ya_scr[qrows, c0:c0 + LANES] = slab[:BLOCK]
            ya_scr[qrows, c0 + LANES:c0 + 2 * LANES] = slab[BLOCK:]

    gq = NA_GROUP_ROWS * GRID_W
    gk = NA_KEY_ROWS * GRID_W
    for grp in range(tq // gq):
        r0 = (t * (tq // gq) + grp) * NA_GROUP_ROWS
        ws = jnp.clip(r0 - NA_ROWS // 2, 0, rows - NA_KEY_ROWS)
        sd = ws - r0
        kbase = pl.multiple_of(ws * GRID_W, GRID_W)
        shift = GRID_W.bit_length() - 1
        r = r0 + (lax.broadcasted_iota(jnp.int32, (gq, 1), 0) >> shift)
        ka = ws + (lax.broadcasted_iota(jnp.int32, (1, gk), 1) >> shift)
        rs = jnp.clip(r - NA_ROWS // 2, 0, rows - NA_ROWS)
        row_mask = jnp.where((ka >= rs) & (ka < rs + NA_ROWS), 0.0, NEG)
        grows = slice(grp * gq, (grp + 1) * gq)
        for a in range(B_HEADS // 2):
            qs = hq_ref[0, grows, A_WIDTH + a * LANES:A_WIDTH + (a + 1) * LANES]
            kd = hkv_ref[0, pl.ds(kbase, gk), KV_KB + a * LANES:KV_KB + (a + 1) * LANES]
            outs = []
            for par, kk in enumerate(split_halves(kd)):
                h = 2 * a + par
                bias_rows = []
                for rq in range(NA_GROUP_ROWS):
                    blocks = []
                    for kp in range(NA_KEY_ROWS // 2):
                        e = jnp.clip(sd + (2 * kp - rq + 8), 0, 16)
                        blocks.append(cb_ref[h, e])
                    bias_rows.append(jnp.concatenate(blocks, axis=1))
                bias = jnp.concatenate(bias_rows, axis=0)
                s = _dot_nt(qs, kk) + bias + row_mask
                m = jnp.max(s, axis=-1, keepdims=True)
                e = jnp.exp(s - m)
                p = e * (1.0 / jnp.sum(e, axis=-1, keepdims=True))
                vd = hkv_ref[0, pl.ds(kbase, gk), KV_VB + h * LANES:KV_VB + (h + 1) * LANES]
                outs.append(_dot(p.astype(BF16), vd))
            yb_scr[grows, a * LANES:(a + 1) * LANES] = jnp.where(low_half, outs[0], outs[1])

    na = _rms_norm(ya_scr[...], ga_ref[...])
    nb_ = _rms_norm(yb_scr[...], gb_ref[...])
    mix = jnp.concatenate([na, nb_], axis=-1).astype(BF16)
    y = alpha * x_ref[0] + _dot(mix, wo_ref[...])
    o_ref[0] = _layer_norm(y, lg_ref[...], lb_ref[...])


def _attn(sink, hq, hkv, x3d, wb, cb, wo, ga, gb, lg, lb, tq, alpha):
    b, seq, d = x3d.shape
    const = lambda *shape: pl.BlockSpec(shape, lambda i, j, s: (0,) * len(shape),
                                        pipeline_mode=pl.Buffered(1))
    grid_spec = pltpu.PrefetchScalarGridSpec(
        num_scalar_prefetch=1,
        grid=(b, seq // tq),
        in_specs=[pl.BlockSpec((1, tq, Q_WIDTH), lambda i, j, s: (i, j, 0)),
                  pl.BlockSpec((1, seq, KV_WIDTH), lambda i, j, s: (i, 0, 0),
                               pipeline_mode=pl.Buffered(1)),
                  pl.BlockSpec((1, tq, d), lambda i, j, s: (i, j, 0)),
                  const(*wb.shape), const(*cb.shape), const(*wo.shape),
                  const(*ga.shape), const(*gb.shape), const(*lg.shape), const(*lb.shape)],
        out_specs=pl.BlockSpec((1, tq, d), lambda i, j, s: (i, j, 0)),
        scratch_shapes=[pltpu.VMEM((tq, A_WIDTH), F32), pltpu.VMEM((tq, B_WIDTH), F32)],
    )
    return pl.pallas_call(
        functools.partial(_attn_kernel, seq=seq, tq=tq, alpha=alpha),
        grid_spec=grid_spec,
        out_shape=jax.ShapeDtypeStruct((b, seq, d), F32),
        compiler_params=pltpu.CompilerParams(dimension_semantics=("parallel", "arbitrary"),
                                             vmem_limit_bytes=VMEM_LIMIT),
        name="attn",
    )(sink, hq, hkv, x3d, wb, cb, wo, ga, gb, lg, lb)


def _route_kernel(x_ref, wq_ref, keys_ref, am_ref, cnt_ref, e1_ref, rk_ref,
                  q_scr, sc_scr, rank_scr, vals_scr, *, tb):
    k = PEER_TOPK
    nk = PEER_N_KEYS
    xt = x_ref[...].T.astype(BF16)
    q_scr[...] = _dot(wq_ref[...], xt).astype(BF16)
    iota_k = lax.broadcasted_iota(jnp.int32, (nk, LANES), 0)

    for lb in range(tb // LANES):
        lanes = slice(lb * LANES, (lb + 1) * LANES)

        def group_body(g, _):
            s0 = _dot(keys_ref[g], q_scr[pl.ds(pl.multiple_of(g * nk, nk), nk), lanes])
            sc_scr[g] = s0
            h = g // 2
            pset = g % 2

            def round_body(a, carry):
                s, rank = carry
                m = jnp.max(s, axis=0, keepdims=True)
                idx = jnp.min(jnp.where(s == m, iota_k, nk), axis=0, keepdims=True)
                hit = iota_k == idx
                vals_scr[pset, pl.ds(a * PEER_HEADS + h, 1), :] = m
                return jnp.where(hit, -jnp.inf, s), jnp.where(hit, a.astype(F32), rank)

            _, rank = lax.fori_loop(0, k, round_body, (s0, jnp.full((nk, LANES), 99.0, F32)))
            rank_scr[g] = rank
            return 0

        lax.fori_loop(0, 2 * PEER_HEADS, group_body, 0)

        v0 = vals_scr[0].reshape(k, PEER_HEADS, LANES)
        v1 = vals_scr[1].reshape(k, PEER_HEADS, LANES)
        iota_a = lax.broadcasted_iota(jnp.int32, (k, PEER_HEADS, LANES), 0).astype(F32)

        def merge_body(_, carry):
            cnt, front = carry
            m = jnp.max(front, axis=0, keepdims=True)
            asel = jnp.min(jnp.where(front == m, iota_a, 99.0), axis=0, keepdims=True)
            hit = iota_a == asel
            cnt = cnt + hit.astype(F32)
            csel = jnp.sum(jnp.where(hit, cnt, 0.0), axis=0, keepdims=True)
            nxt = jnp.sum(jnp.where(iota_a == csel, v1, 0.0), axis=0, keepdims=True)
            nxt = jnp.where(csel >= k, -jnp.inf, nxt)
            return cnt, jnp.where(hit, v0 + nxt, front)

        cnt, _ = lax.fori_loop(0, k, merge_body,
                               (jnp.zeros((k, PEER_HEADS, LANES), F32), v0 + v1[0:1]))

        e0 = jnp.exp(v0 - v0[0:1])
        e1 = jnp.exp(v1 - v1[0:1])
        pref = jnp.zeros((k, PEER_HEADS, LANES), F32)
        for b in range(k):
            pref = pref + jnp.where(cnt > b, e1[b:b + 1], 0.0)
        inv_z = 1.0 / jnp.sum(e0 * pref, axis=0)

        for h in range(PEER_HEADS):
            rank0 = rank_scr[2 * h]
            cnt_map = jnp.zeros((nk, LANES), F32)
            for a in range(k):
                cnt_map = cnt_map + jnp.where(rank0 == a, cnt[a, h:h + 1, :], 0.0)
            gate0 = jnp.exp(sc_scr[2 * h] - v0[0, h:h + 1, :]) * inv_z[h:h + 1, :]
            am_ref[h, :, lanes] = jnp.where(rank0 < k, gate0, 0.0)
            cnt_ref[h, :, lanes] = cnt_map
            e1_ref[h, :, lanes] = jnp.exp(sc_scr[2 * h + 1] - v1[0, h:h + 1, :])
            rk_ref[h, :, lanes] = rank_scr[2 * h + 1]


def _route(x1, wq_t, keys, tb):
    n, d = x1.shape
    nk = PEER_N_KEYS
    out = jax.ShapeDtypeStruct((PEER_HEADS, nk, n), F32)
    ospec = pl.BlockSpec((PEER_HEADS, nk, tb), lambda i: (0, 0, i))
    return pl.pallas_call(
        functools.partial(_route_kernel, tb=tb),
        grid=(n // tb,),
        in_specs=[pl.BlockSpec((tb, d), lambda i: (i, 0)),
                  pl.BlockSpec(wq_t.shape, lambda i: (0, 0)),
                  pl.BlockSpec(keys.shape, lambda i: (0, 0, 0))],
        out_specs=[ospec, ospec, ospec, ospec],
        out_shape=[out, out, out, out],
        scratch_shapes=[pltpu.VMEM((2 * PEER_HEADS * nk, tb), BF16),
                        pltpu.VMEM((2 * PEER_HEADS, nk, LANES), F32),
                        pltpu.VMEM((2 * PEER_HEADS, nk, LANES), F32),
                        pltpu.VMEM((2, PEER_TOPK * PEER_HEADS, LANES), F32)],
        compiler_params=pltpu.CompilerParams(dimension_semantics=("parallel",),
                                             vmem_limit_bytes=VMEM_LIMIT),
        name="peer_route",
    )(x1, wq_t, keys)


def _dense_kernel(x_ref, am_ref, cnt_ref, e1_ref, rk_ref, u_ref, vt_ref, lg_ref, lb_ref, o_ref,
                  xt_scr, h_scr, a_scr, acc_scr, *, tb, ec, alpha):
    c = pl.program_id(1)
    nk = PEER_N_KEYS

    @pl.when(c == 0)
    def _():
        xt_scr[...] = x_ref[...].T.astype(BF16)
        acc_scr[...] = jnp.zeros_like(acc_scr)

    h_scr[...] = _dot(u_ref[...], xt_scr[...])

    for ii in range(ec // nk):
        rows = slice(ii * nk, (ii + 1) * nk)
        for lb in range(tb // LANES):
            lanes = slice(lb * LANES, (lb + 1) * LANES)
            gate = jnp.zeros((nk, LANES), F32)
            for h in range(PEER_HEADS):
                am = am_ref[h, ii:ii + 1, lanes]
                cnt = cnt_ref[h, ii:ii + 1, lanes]
                gate = gate + jnp.where(rk_ref[h, :, lanes] < cnt, am * e1_ref[h, :, lanes], 0.0)
            hv = h_scr[rows, lanes]
            half = 0.5 * hv
            gelu = half + half * lax.erf(hv * (0.5 ** 0.5))
            a_scr[rows, lanes] = (gelu * gate).astype(BF16)

    acc_scr[...] += _dot(vt_ref[...], a_scr[...])

    @pl.when(c == pl.num_programs(1) - 1)
    def _():
        y = alpha * x_ref[...] + acc_scr[...].T
        o_ref[...] = _layer_norm(y, lg_ref[...], lb_ref[...])


def _dense(x1, am, cnt, e1, rk, u, vt, lg, lb, tb, ec, alpha):
    n, d = x1.shape
    n_exp = u.shape[0]
    nk = PEER_N_KEYS
    rspec = pl.BlockSpec((PEER_HEADS, nk, tb), lambda i, c: (0, 0, i))
    cspec = pl.BlockSpec((PEER_HEADS, ec // nk, tb), lambda i, c: (0, c, i))
    return pl.pallas_call(
        functools.partial(_dense_kernel, tb=tb, ec=ec, alpha=alpha),
        grid=(n // tb, n_exp // ec),
        in_specs=[pl.BlockSpec((tb, d), lambda i, c: (i, 0)),
                  cspec, cspec, rspec, rspec,
                  pl.BlockSpec((ec, d), lambda i, c: (c, 0)),
                  pl.BlockSpec((d, ec), lambda i, c: (0, c)),
                  pl.BlockSpec(lg.shape, lambda i, c: (0, 0)),
                  pl.BlockSpec(lb.shape, lambda i, c: (0, 0))],
        out_specs=pl.BlockSpec((tb, d), lambda i, c: (i, 0)),
        out_shape=jax.ShapeDtypeStruct((n, d), F32),
        scratch_shapes=[pltpu.VMEM((d, tb), BF16),
                        pltpu.VMEM((ec, tb), F32),
                        pltpu.VMEM((ec, tb), BF16),
                        pltpu.VMEM((d, tb), F32)],
        compiler_params=pltpu.CompilerParams(dimension_semantics=("parallel", "arbitrary"),
                                             vmem_limit_bytes=VMEM_LIMIT),
        name="peer_dense",
    )(x1, am, cnt, e1, rk, u, vt, lg, lb)


def kernel(x, w_in, w_o, attn_sink, na_rpb, t5_table, gnorm_a, gnorm_b, ln1_g, ln1_b, ln2_g, ln2_b,
           peer_wq, peer_keys, peer_u, peer_v):
    b, seq, d = x.shape
    depth = w_in.shape[0]
    n = b * seq
    alpha = (2 * depth) ** 0.25
    tq = min(512, seq)
    tm = min(512, n)
    tb_route = min(256, n)
    tb_dense = min(512, n)
    ec = 1024
    assert seq % tq == 0 and seq // GRID_W >= NA_KEY_ROWS and seq // BLOCK >= 3

    wb = _window_bias_table(t5_table)
    cb_all = _na_bias_table(na_rpb)
    row = lambda v: v.reshape(1, -1)

    for l in range(depth):
        hq, hkv = _proj(x.reshape(n, d), _proj_weight(w_in[l]), tm)
        x1 = _attn(attn_sink[l], hq.reshape(b, seq, Q_WIDTH), hkv.reshape(b, seq, KV_WIDTH), x,
                   wb, cb_all[l], w_o[l].astype(BF16), row(gnorm_a[l]), row(gnorm_b[l]),
                   row(ln1_g[l]), row(ln1_b[l]), tq, alpha)
        x1 = x1.reshape(n, d)
        keys = peer_keys[l].reshape(2 * PEER_HEADS, PEER_N_KEYS, PEER_QDIM // 2).astype(BF16)
        am, cnt, e1, rk = _route(x1, peer_wq[l].T.astype(BF16), keys, tb_route)
        x = _dense(x1, am, cnt, e1, rk, peer_u[l].astype(BF16), peer_v[l].T.astype(BF16),
                   row(ln2_g[l]), row(ln2_b[l]), tb_dense, ec, alpha).reshape(b, seq, d)
    return x
```

```python
import functools
import math

import numpy as np
import jax
import jax.numpy as jnp
from jax import lax
from jax.experimental import pallas as pl
from jax.experimental.pallas import tpu as pltpu

F32 = jnp.float32
BF16 = jnp.bfloat16

HEAD_DIM = 64
A_HEADS = 8
A_KV_HEADS = 2
B_HEADS = 8
A_WIDTH = A_HEADS * HEAD_DIM
A_KV_WIDTH = A_KV_HEADS * HEAD_DIM
B_WIDTH = B_HEADS * HEAD_DIM
WINDOW = 128
BLOCK = 128
GRID_W = 64
NA_ROWS = 8
NA_COLS = 16
T5_BUCKETS = 32
T5_MAX_DIST = 128
PEER_HEADS = 8
PEER_N_KEYS = 128
PEER_TOPK = 16
PEER_QDIM = 256
LN_EPS = 1e-5
NEG = -1e30

LANES = 128
BF16_ROWS = 16
NA_GROUP_ROWS = 4
NA_KEY_ROWS = 12
Q_WIDTH = A_WIDTH + B_WIDTH
KV_WIDTH = 2 * A_KV_WIDTH + 2 * A_KV_WIDTH + B_WIDTH + 2 * B_WIDTH
KV_KA, KV_VA, KV_KB, KV_VB = 0, 2 * A_KV_WIDTH, 4 * A_KV_WIDTH, 4 * A_KV_WIDTH + B_WIDTH

VMEM_LIMIT = 56 * 1024 * 1024


def _dot(a, b):
    return jnp.dot(a, b, preferred_element_type=F32)


def _dot_nt(a, b):
    return lax.dot_general(a, b, (((1,), (1,)), ((), ())), preferred_element_type=F32)


def _layer_norm(y, g, b):
    mu = jnp.mean(y, axis=-1, keepdims=True)
    d = y - mu
    var = jnp.mean(d * d, axis=-1, keepdims=True)
    return d * lax.rsqrt(var + LN_EPS) * g + b


def _rms_norm(y, g):
    return y * lax.rsqrt(jnp.mean(y * y, axis=-1, keepdims=True) + LN_EPS) * g


def _lookup_kernel(tab_ref, idx_ref, mask_ref, o_ref):
    n_entries = tab_ref.shape[1]
    width = idx_ref.shape[1]
    onehot = (lax.broadcasted_iota(jnp.int32, (n_entries, width), 0) == idx_ref[...]).astype(F32)
    val = jnp.dot(tab_ref[...], onehot, preferred_element_type=F32, precision=lax.Precision.HIGHEST)
    o_ref[...] = jnp.where(mask_ref[...] > 0, val, NEG)


def _lookup(tab, idx, mask, chunk):
    rows, n_entries = tab.shape
    width = idx.shape[1]
    return pl.pallas_call(
        _lookup_kernel,
        grid=(width // chunk,),
        in_specs=[pl.BlockSpec((rows, n_entries), lambda i: (0, 0)),
                  pl.BlockSpec((1, chunk), lambda i: (0, i)),
                  pl.BlockSpec((1, chunk), lambda i: (0, i))],
        out_specs=pl.BlockSpec((rows, chunk), lambda i: (0, i)),
        out_shape=jax.ShapeDtypeStruct((rows, width), F32),
        name="bias_lookup",
    )(tab, idx, mask)


def _t5_bucket(rel):
    nb = T5_BUCKETS // 2
    max_exact = nb // 2
    ret = (rel > 0).astype(jnp.int32) * nb
    n = jnp.abs(rel).astype(jnp.int32)
    nf = jnp.maximum(n, 1).astype(jnp.float32)
    large = max_exact + (jnp.log(nf / max_exact) / math.log(T5_MAX_DIST / max_exact)
                         * (nb - max_exact)).astype(jnp.int32)
    large = jnp.minimum(large, nb - 1)
    return ret + jnp.where(n < max_exact, n, large)


def _window_bias_table(t5_table):
    q = jnp.arange(BLOCK)[:, None]
    s = jnp.arange(3 * BLOCK)[None, :]
    rel = jnp.stack([s - q, s - BLOCK - q, s - 2 * BLOCK - q])
    idx = _t5_bucket(rel).reshape(1, -1)
    mask = (jnp.abs(rel) <= WINDOW).astype(F32).reshape(1, -1)
    tab = _lookup(t5_table.T.astype(F32), idx, mask, 3 * BLOCK * 32)
    tab = tab.reshape(A_KV_HEADS, 2, 2, 3, BLOCK, 3 * BLOCK)
    tab = tab.transpose(3, 0, 2, 1, 4, 5)
    return tab.reshape(3, A_KV_HEADS, 2, 2 * BLOCK, 3 * BLOCK)


def _na_bias_table(na_rpb):
    depth = na_rpb.shape[0]
    n_dr, n_dc = 2 * NA_ROWS - 1, 2 * NA_COLS - 1
    c = jnp.arange(GRID_W)
    col_start = jnp.clip(c - NA_COLS // 2, 0, GRID_W - NA_COLS)
    col_mask = (c[None, :] >= col_start[:, None]) & (c[None, :] < col_start[:, None] + NA_COLS)
    dc = jnp.clip(c[None, :] - c[:, None], -(NA_COLS - 1), NA_COLS - 1) + (NA_COLS - 1)
    tab = jnp.pad(na_rpb.reshape(depth * B_HEADS * n_dr, n_dc).astype(F32), ((0, 0), (0, 1)))
    out = _lookup(tab, dc.reshape(1, -1).astype(jnp.int32), col_mask.astype(F32).reshape(1, -1),
                  GRID_W * GRID_W)
    out = out.reshape(depth, B_HEADS, n_dr, GRID_W, GRID_W)
    out = jnp.pad(out, ((0, 0), (0, 0), (1, 2), (0, 0), (0, 0)))
    return jnp.concatenate([out[:, :, :-1], out[:, :, 1:]], axis=-1)


def _proj_kernel(x_ref, w_ref, hq_ref, hkv_ref):
    h = _dot(x_ref[...].astype(BF16), w_ref[...])
    hq_ref[...] = (h[:, :Q_WIDTH] * (HEAD_DIM ** -0.5)).astype(BF16)
    hkv_ref[...] = h[:, Q_WIDTH:].astype(BF16)


def _proj(x2d, w_cat, tm):
    n, d = x2d.shape
    return pl.pallas_call(
        _proj_kernel,
        grid=(n // tm,),
        in_specs=[pl.BlockSpec((tm, d), lambda i: (i, 0)),
                  pl.BlockSpec(w_cat.shape, lambda i: (0, 0))],
        out_specs=[pl.BlockSpec((tm, Q_WIDTH), lambda i: (i, 0)),
                   pl.BlockSpec((tm, KV_WIDTH), lambda i: (i, 0))],
        out_shape=[jax.ShapeDtypeStruct((n, Q_WIDTH), BF16),
                   jax.ShapeDtypeStruct((n, KV_WIDTH), BF16)],
        compiler_params=pltpu.CompilerParams(dimension_semantics=("parallel",),
                                             vmem_limit_bytes=VMEM_LIMIT),
        name="proj",
    )(x2d, w_cat)


def _proj_weight(w_in_l):
    o1 = A_WIDTH
    o2 = o1 + A_KV_WIDTH
    o3 = o2 + A_KV_WIDTH
    o4 = o3 + B_WIDTH
    o5 = o4 + B_WIDTH

    def dup(w, heads):
        w = w.reshape(w.shape[0], heads, 1, HEAD_DIM)
        return jnp.broadcast_to(w, (w.shape[0], heads, 2, HEAD_DIM)).reshape(w.shape[0], heads * 2 * HEAD_DIM)

    cols = [w_in_l[:, :o1], w_in_l[:, o3:o4],
            dup(w_in_l[:, o1:o2], A_KV_HEADS), dup(w_in_l[:, o2:o3], A_KV_HEADS),
            w_in_l[:, o4:o5], dup(w_in_l[:, o5:], B_HEADS)]
    return jnp.concatenate(cols, axis=1).astype(BF16)


def _attn_kernel(sink_ref, hq_ref, hkv_ref, x_ref, wb_ref, cb_ref, wo_ref, ga_ref, gb_ref,
                 lg_ref, lb_ref, o_ref, ya_scr, yb_scr, *, seq, tq, alpha):
    t = pl.program_id(1)
    n_blocks = seq // BLOCK
    rows = seq // GRID_W
    low_half = lax.broadcasted_iota(jnp.int32, (1, LANES), 1) < HEAD_DIM

    def split_halves(kd):
        zero = jnp.zeros_like(kd)
        return jnp.where(low_half, kd, zero), jnp.where(low_half, zero, kd)

    top_rows = lax.broadcasted_iota(jnp.int32, (2 * BLOCK, 1), 0) < BLOCK
    for j in range(tq // BLOCK):
        nb = t * (tq // BLOCK) + j
        kstart = pl.multiple_of(jnp.clip(nb - 1, 0, n_blocks - 3) * BLOCK, BLOCK)
        var = jnp.where(nb == 0, 0, jnp.where(nb == n_blocks - 1, 2, 1))
        qrows = slice(j * BLOCK, (j + 1) * BLOCK)
        for kv in range(A_KV_HEADS):
            c0 = kv * 2 * LANES
            q2 = jnp.concatenate([hq_ref[0, qrows, c0:c0 + LANES],
                                  hq_ref[0, qrows, c0 + LANES:c0 + 2 * LANES]], axis=0)
            kd = hkv_ref[0, pl.ds(kstart, 3 * BLOCK), KV_KA + kv * LANES:KV_KA + (kv + 1) * LANES]
            vd = hkv_ref[0, pl.ds(kstart, 3 * BLOCK), KV_VA + kv * LANES:KV_VA + (kv + 1) * LANES]
            outs = []
            for par, kk in enumerate(split_halves(kd)):
                s = _dot_nt(q2, kk) + wb_ref[var, kv, par]
                sk = jnp.where(top_rows, sink_ref[4 * kv + par], sink_ref[4 * kv + par + 2])
                m = jnp.maximum(jnp.max(s, axis=-1, keepdims=True), sk)
                e = jnp.exp(s - m)
                den = jnp.sum(e, axis=-1, keepdims=True) + jnp.exp(sk - m)
                p = e * (1.0 / den)
                outs.append(_dot(p.astype(BF16), vd))
            slab = jnp.where(low_half, outs[0], outs[1])
            ya_scr[qrows, c0:c0 + LANES] = slab[:BLOCK]
            ya_scr[qrows, c0 + LANES:c0 + 2 * LANES] = slab[BLOCK:]

    gq = NA_GROUP_ROWS * GRID_W
    gk = NA_KEY_ROWS * GRID_W
    for grp in range(tq // gq):
        r0 = (t * (tq // gq) + grp) * NA_GROUP_ROWS
        ws = jnp.clip(r0 - NA_ROWS // 2, 0, rows - NA_KEY_ROWS)
        sd = ws - r0
        kbase = pl.multiple_of(ws * GRID_W, GRID_W)
        shift = GRID_W.bit_length() - 1
        r = r0 + (lax.broadcasted_iota(jnp.int32, (gq, 1), 0) >> shift)
        ka = ws + (lax.broadcasted_iota(jnp.int32, (1, gk), 1) >> shift)
        rs = jnp.clip(r - NA_ROWS // 2, 0, rows - NA_ROWS)
        row_mask = jnp.where((ka >= rs) & (ka < rs + NA_ROWS), 0.0, NEG)
        grows = slice(grp * gq, (grp + 1) * gq)
        for a in range(B_HEADS // 2):
            qs = hq_ref[0, grows, A_WIDTH + a * LANES:A_WIDTH + (a + 1) * LANES]
            kd = hkv_ref[0, pl.ds(kbase, gk), KV_KB + a * LANES:KV_KB + (a + 1) * LANES]
            outs = []
            for par, kk in enumerate(split_halves(kd)):
                h = 2 * a + par
                bias_rows = []
                for rq in range(NA_GROUP_ROWS):
                    blocks = []
                    for kp in range(NA_KEY_ROWS // 2):
                        e = jnp.clip(sd + (2 * kp - rq + 8), 0, 16)
                        blocks.append(cb_ref[h, e])
                    bias_rows.append(jnp.concatenate(blocks, axis=1))
                bias = jnp.concatenate(bias_rows, axis=0)
                s = _dot_nt(qs, kk) + bias + row_mask
                m = jnp.max(s, axis=-1, keepdims=True)
                e = jnp.exp(s - m)
                p = e * (1.0 / jnp.sum(e, axis=-1, keepdims=True))
                vd = hkv_ref[0, pl.ds(kbase, gk), KV_VB + h * LANES:KV_VB + (h + 1) * LANES]
                outs.append(_dot(p.astype(BF16), vd))
            yb_scr[grows, a * LANES:(a + 1) * LANES] = jnp.where(low_half, outs[0], outs[1])

    na = _rms_norm(ya_scr[...], ga_ref[...])
    nb_ = _rms_norm(yb_scr[...], gb_ref[...])
    mix = jnp.concatenate([na, nb_], axis=-1).astype(BF16)
    y = alpha * x_ref[0] + _dot(mix, wo_ref[...])
    o_ref[0] = _layer_norm(y, lg_ref[...], lb_ref[...])


def _attn(sink, hq, hkv, x3d, wb, cb, wo, ga, gb, lg, lb, tq, alpha):
    b, seq, d = x3d.shape
    const = lambda *shape: pl.BlockSpec(shape, lambda i, j, s: (0,) * len(shape),
                                        pipeline_mode=pl.Buffered(1))
    grid_spec = pltpu.PrefetchScalarGridSpec(
        num_scalar_prefetch=1,
        grid=(b, seq // tq),
        in_specs=[pl.BlockSpec((1, tq, Q_WIDTH), lambda i, j, s: (i, j, 0)),
                  pl.BlockSpec((1, seq, KV_WIDTH), lambda i, j, s: (i, 0, 0),
                               pipeline_mode=pl.Buffered(1)),
                  pl.BlockSpec((1, tq, d), lambda i, j, s: (i, j, 0)),
                  const(*wb.shape), const(*cb.shape), const(*wo.shape),
                  const(*ga.shape), const(*gb.shape), const(*lg.shape), const(*lb.shape)],
        out_specs=pl.BlockSpec((1, tq, d), lambda i, j, s: (i, j, 0)),
        scratch_shapes=[pltpu.VMEM((tq, A_WIDTH), F32), pltpu.VMEM((tq, B_WIDTH), F32)],
    )
    return pl.pallas_call(
        functools.partial(_attn_kernel, seq=seq, tq=tq, alpha=alpha),
        grid_spec=grid_spec,
        out_shape=jax.ShapeDtypeStruct((b, seq, d), F32),
        compiler_params=pltpu.CompilerParams(dimension_semantics=("parallel", "arbitrary"),
                                             vmem_limit_bytes=VMEM_LIMIT),
        name="attn",
    )(sink, hq, hkv, x3d, wb, cb, wo, ga, gb, lg, lb)


def _tree(op, xs):
    xs = list(xs)
    while len(xs) > 1:
        xs = [op(xs[i], xs[i + 1]) for i in range(0, len(xs) - 1, 2)] + ([xs[-1]] if len(xs) % 2 else [])
    return xs[0]


def _route_kernel(x_ref, wq_ref, kbd_ref, am_ref, cnt_ref, e1_ref, rk_ref,
                  q_scr, s_scr, w_scr, rank_scr, vals_scr, e_scr, *, tb):
    k = PEER_TOPK
    nk = PEER_N_KEYS
    nh = PEER_HEADS
    half = nh * (PEER_QDIM // 2)
    xt = x_ref[...].T.astype(BF16)
    q_scr[...] = _dot(wq_ref[...], xt).astype(BF16)
    for p in range(2):
        s_scr[p] = _dot(kbd_ref[p], q_scr[p * half:(p + 1) * half, :])

    def slab(kk):
        return slice(kk * nh, (kk + 1) * nh)

    for lb in range(tb // LANES):
        lanes = slice(lb * LANES, (lb + 1) * LANES)

        for p in range(2):
            w_scr[...] = s_scr[p, :, lanes]
            rank_scr[p] = jnp.full((nk * nh, LANES), 99.0, F32)

            def round_body(a, _, p=p):
                vals = [w_scr[slab(kk), :] for kk in range(nk)]
                m = _tree(jnp.maximum, vals)
                idx = _tree(jnp.minimum, [jnp.where(v == m, float(kk), float(nk))
                                          for kk, v in enumerate(vals)])
                vals_scr[p, pl.ds(pl.multiple_of(a * nh, nh), nh), :] = m
                af = a.astype(F32)
                for kk in range(nk):
                    hit = idx == float(kk)
                    w_scr[slab(kk), :] = jnp.where(hit, -jnp.inf, vals[kk])
                    rank_scr[p, slab(kk), :] = jnp.where(hit, af, rank_scr[p, slab(kk), :])
                return 0

            lax.fori_loop(0, k, round_body, 0)

        v0 = vals_scr[0].reshape(k, nh, LANES)
        v1 = vals_scr[1].reshape(k, nh, LANES)
        iota_a = lax.broadcasted_iota(jnp.int32, (k, nh, LANES), 0).astype(F32)

        def merge_body(_, carry):
            cnt, front = carry
            m = jnp.max(front, axis=0, keepdims=True)
            asel = jnp.min(jnp.where(front == m, iota_a, 99.0), axis=0, keepdims=True)
            hit = iota_a == asel
            cnt = cnt + hit.astype(F32)
            csel = jnp.sum(jnp.where(hit, cnt, 0.0), axis=0, keepdims=True)
            nxt = jnp.sum(jnp.where(iota_a == csel, v1, 0.0), axis=0, keepdims=True)
            nxt = jnp.where(csel >= k, -jnp.inf, nxt)
            return cnt, jnp.where(hit, v0 + nxt, front)

        cnt, _ = lax.fori_loop(0, k, merge_body, (jnp.zeros((k, nh, LANES), F32), v0 + v1[0:1]))

        e0 = jnp.exp(v0 - v0[0:1])
        e1 = jnp.exp(v1 - v1[0:1])
        pref = jnp.zeros((k, nh, LANES), F32)
        for b in range(k):
            pref = pref + jnp.where(cnt > b, e1[b:b + 1], 0.0)
        inv_z = 1.0 / jnp.sum(e0 * pref, axis=0)

        for kk in range(nk):
            rank0 = rank_scr[0, slab(kk), :]
            gate0 = jnp.exp(s_scr[0, slab(kk), lanes] - v0[0]) * inv_z
            am_ref[kk, :, lanes] = jnp.where(rank0 < k, gate0, 0.0)
            cnt_ref[kk, :, lanes] = _tree(jnp.add, [jnp.where(rank0 == a, cnt[a], 0.0) for a in range(k)])
            e_scr[slab(kk), :] = jnp.exp(s_scr[1, slab(kk), lanes] - v1[0])
        for h in range(nh):
            e1_ref[h, :, lanes] = e_scr[pl.ds(h, nk, stride=nh), :].astype(BF16)
            rk_ref[h, :, lanes] = rank_scr[1, pl.ds(h, nk, stride=nh), :].astype(BF16)


def _route(x1, wq_t, kbd, tb):
    n, d = x1.shape
    nk = PEER_N_KEYS
    nh = PEER_HEADS
    by_key = jax.ShapeDtypeStruct((nk, nh, n), F32)
    by_head = jax.ShapeDtypeStruct((nh, nk, n), BF16)
    kspec = pl.BlockSpec((nk, nh, tb), lambda i: (0, 0, i))
    hspec = pl.BlockSpec((nh, nk, tb), lambda i: (0, 0, i))
    return pl.pallas_call(
        functools.partial(_route_kernel, tb=tb),
        grid=(n // tb,),
        in_specs=[pl.BlockSpec((tb, d), lambda i: (i, 0)),
                  pl.BlockSpec(wq_t.shape, lambda i: (0, 0)),
                  pl.BlockSpec(kbd.shape, lambda i: (0, 0, 0))],
        out_specs=[kspec, kspec, hspec, hspec],
        out_shape=[by_key, by_key, by_head, by_head],
        scratch_shapes=[pltpu.VMEM((2 * nh * nk, tb), BF16),
                        pltpu.VMEM((2, nk * nh, tb), F32),
                        pltpu.VMEM((nk * nh, LANES), F32),
                        pltpu.VMEM((2, nk * nh, LANES), F32),
                        pltpu.VMEM((2, PEER_TOPK * nh, LANES), F32),
                        pltpu.VMEM((nk * nh, LANES), F32)],
        compiler_params=pltpu.CompilerParams(dimension_semantics=("parallel",),
                                             vmem_limit_bytes=VMEM_LIMIT),
        name="peer_route",
    )(x1, wq_t, kbd)


def _route_weights(peer_wq_l, peer_keys_l):
    nh, nk, c = PEER_HEADS, PEER_N_KEYS, PEER_QDIM // 2
    d = peer_wq_l.shape[0]
    wq_t = peer_wq_l.T.reshape(nh, 2, c, d).transpose(1, 0, 2, 3).reshape(2 * nh * c, d)
    kbd = jnp.einsum('hpkc,hg->pkhgc', peer_keys_l, jnp.eye(nh, dtype=peer_keys_l.dtype))
    return wq_t.astype(BF16), kbd.reshape(2, nk * nh, nh * c).astype(BF16)


def _dense_kernel(xa_ref, xb_ref, am_ref, cnt_ref, e1_ref, rk_ref, u_ref, vt_ref, lg_ref, lb_ref,
                  o_ref, xt_scr, h0_scr, h1_scr, a0_scr, a1_scr, acc_scr, *, tb, ec, nch, alpha):
    s = pl.program_id(0)
    nk = PEER_N_KEYS
    phase = s % nch
    h_bufs = (h0_scr, h1_scr)
    a_bufs = (a0_scr, a1_scr)

    @pl.when(s == 0)
    def _():
        for buf in h_bufs + a_bufs + (acc_scr,):
            buf[...] = jnp.zeros_like(buf)

    @pl.when(phase == 0)
    def _():
        xt_scr[...] = xa_ref[...].T.astype(BF16)

    @pl.when(phase == 2 % nch)
    def _():
        acc_scr[...] = jnp.zeros_like(acc_scr)

    def stage(cur):
        h_cur, h_prv = h_bufs[cur], h_bufs[1 - cur]
        a_cur, a_prv = a_bufs[cur], a_bufs[1 - cur]
        acc_scr[...] += _dot(vt_ref[...], a_cur[...])
        h_cur[...] = _dot(u_ref[...], xt_scr[...])
        for ii in range(ec // nk):
            rows = slice(ii * nk, (ii + 1) * nk)
            for lb in range(tb // LANES):
                lanes = slice(lb * LANES, (lb + 1) * LANES)
                packed = (nk // BF16_ROWS, BF16_ROWS, LANES)
                gate = jnp.zeros(packed, BF16)
                for h in range(PEER_HEADS):
                    am = jnp.broadcast_to(am_ref[ii, h:h + 1, lanes], packed[1:]).astype(BF16)
                    cnt = jnp.broadcast_to(cnt_ref[ii, h:h + 1, lanes], packed[1:]).astype(BF16)
                    rk = rk_ref[h, :, lanes].reshape(packed)
                    e1 = e1_ref[h, :, lanes].reshape(packed)
                    gate = gate + jnp.where(cnt[None] - rk > 0, am[None] * e1, jnp.zeros((), BF16))
                hv = h_prv[rows, lanes]
                half = 0.5 * hv
                gelu = half + half * lax.erf(hv * (0.5 ** 0.5))
                a_prv[rows, lanes] = (gelu.astype(BF16).reshape(packed) * gate).reshape(nk, LANES)

    pl.when(s % 2 == 0)(lambda: stage(0))
    pl.when(s % 2 == 1)(lambda: stage(1))

    @pl.when((phase == 1 % nch) & (s > 1))
    def _():
        y = alpha * xb_ref[...] + acc_scr[...].T
        o_ref[...] = _layer_norm(y, lg_ref[...], lb_ref[...])


def _dense(x1, am, cnt, e1, rk, u, vt, lg, lb, tb, ec, alpha):
    n, d = x1.shape
    n_exp = u.shape[0]
    nk = PEER_N_KEYS
    nch = n_exp // ec
    assert nch % 2 == 0 and nch >= 4
    total = (n // tb) * nch

    def m1(s):
        return jnp.minimum(s, total - 1)

    def gt(s):
        return jnp.clip(s - 1, 0, total - 1)

    def m2(s):
        return jnp.clip(s - 2, 0, total - 1)

    rspec = pl.BlockSpec((PEER_HEADS, nk, tb), lambda s: (0, 0, gt(s) // nch))
    cspec = pl.BlockSpec((ec // nk, PEER_HEADS, tb), lambda s: (gt(s) % nch, 0, gt(s) // nch))
    return pl.pallas_call(
        functools.partial(_dense_kernel, tb=tb, ec=ec, nch=nch, alpha=alpha),
        grid=(total + 2,),
        in_specs=[pl.BlockSpec((tb, d), lambda s: (m1(s) // nch, 0)),
                  pl.BlockSpec((tb, d), lambda s: (m2(s) // nch, 0)),
                  cspec, cspec, rspec, rspec,
                  pl.BlockSpec((ec, d), lambda s: (m1(s) % nch, 0)),
                  pl.BlockSpec((d, ec), lambda s: (0, m2(s) % nch)),
                  pl.BlockSpec(lg.shape, lambda s: (0, 0)),
                  pl.BlockSpec(lb.shape, lambda s: (0, 0))],
        out_specs=pl.BlockSpec((tb, d), lambda s: (m2(s) // nch, 0)),
        out_shape=jax.ShapeDtypeStruct((n, d), F32),
        scratch_shapes=[pltpu.VMEM((d, tb), BF16),
                        pltpu.VMEM((ec, tb), F32), pltpu.VMEM((ec, tb), F32),
                        pltpu.VMEM((ec, tb), BF16), pltpu.VMEM((ec, tb), BF16),
                        pltpu.VMEM((d, tb), F32)],
        compiler_params=pltpu.CompilerParams(dimension_semantics=("arbitrary",),
                                             vmem_limit_bytes=VMEM_LIMIT),
        name="peer_dense",
    )(x1, x1, am, cnt, e1, rk, u, vt, lg, lb)


def kernel(x, w_in, w_o, attn_sink, na_rpb, t5_table, gnorm_a, gnorm_b, ln1_g, ln1_b, ln2_g, ln2_b,
           peer_wq, peer_keys, peer_u, peer_v):
    b, seq, d = x.shape
    depth = w_in.shape[0]
    n = b * seq
    alpha = (2 * depth) ** 0.25
    tq = min(512, seq)
    tm = min(512, n)
    tb_route = min(256, n)
    tb_dense = min(512, n)
    ec = 1024
    assert seq % tq == 0 and seq // GRID_W >= NA_KEY_ROWS and seq // BLOCK >= 3

    wb = _window_bias_table(t5_table)
    cb_all = _na_bias_table(na_rpb)
    row = lambda v: v.reshape(1, -1)

    for l in range(depth):
        hq, hkv = _proj(x.reshape(n, d), _proj_weight(w_in[l]), tm)
        x1 = _attn(attn_sink[l], hq.reshape(b, seq, Q_WIDTH), hkv.reshape(b, seq, KV_WIDTH), x,
                   wb, cb_all[l], w_o[l].astype(BF16), row(gnorm_a[l]), row(gnorm_b[l]),
                   row(ln1_g[l]), row(ln1_b[l]), tq, alpha)
        x1 = x1.reshape(n, d)
        wq_t, kbd = _route_weights(peer_wq[l], peer_keys[l])
        am, cnt, e1, rk = _route(x1, wq_t, kbd, tb_route)
        x = _dense(x1, am, cnt, e1, rk, peer_u[l].astype(BF16), peer_v[l].T.astype(BF16),
                   row(ln2_g[l]), row(ln2_b[l]), tb_dense, ec, alpha).reshape(b, seq, d)
    return x
```

```python
import functools
import math

import numpy as np
import jax
import jax.numpy as jnp
from jax import lax
from jax.experimental import pallas as pl
from jax.experimental.pallas import tpu as pltpu

F32 = jnp.float32
BF16 = jnp.bfloat16

HEAD_DIM = 64
A_HEADS = 8
A_KV_HEADS = 2
B_HEADS = 8
A_WIDTH = A_HEADS * HEAD_DIM
A_KV_WIDTH = A_KV_HEADS * HEAD_DIM
B_WIDTH = B_HEADS * HEAD_DIM
WINDOW = 128
BLOCK = 128
GRID_W = 64
NA_ROWS = 8
NA_COLS = 16
T5_BUCKETS = 32
T5_MAX_DIST = 128
PEER_HEADS = 8
PEER_N_KEYS = 128
PEER_TOPK = 16
PEER_QDIM = 256
LN_EPS = 1e-5
NEG = -1e30

LANES = 128
BF16_ROWS = 16
DENSE_PARTS = 1
NA_GROUP_ROWS = 4
NA_KEY_ROWS = 12
Q_WIDTH = A_WIDTH + B_WIDTH
KV_WIDTH = 2 * A_KV_WIDTH + 2 * A_KV_WIDTH + B_WIDTH + 2 * B_WIDTH
KV_KA, KV_VA, KV_KB, KV_VB = 0, 2 * A_KV_WIDTH, 4 * A_KV_WIDTH, 4 * A_KV_WIDTH + B_WIDTH

VMEM_LIMIT = 56 * 1024 * 1024


def _dot(a, b):
    return jnp.dot(a, b, preferred_element_type=F32)


def _dot_nt(a, b):
    return lax.dot_general(a, b, (((1,), (1,)), ((), ())), preferred_element_type=F32)


def _layer_norm(y, g, b):
    mu = jnp.mean(y, axis=-1, keepdims=True)
    d = y - mu
    var = jnp.mean(d * d, axis=-1, keepdims=True)
    return d * lax.rsqrt(var + LN_EPS) * g + b


def _rms_norm(y, g):
    return y * lax.rsqrt(jnp.mean(y * y, axis=-1, keepdims=True) + LN_EPS) * g


def _lookup_kernel(tab_ref, idx_ref, mask_ref, o_ref):
    n_entries = tab_ref.shape[1]
    width = idx_ref.shape[1]
    onehot = (lax.broadcasted_iota(jnp.int32, (n_entries, width), 0) == idx_ref[...]).astype(F32)
    val = jnp.dot(tab_ref[...], onehot, preferred_element_type=F32, precision=lax.Precision.HIGHEST)
    o_ref[...] = jnp.where(mask_ref[...] > 0, val, NEG)


def _lookup(tab, idx, mask, chunk):
    rows, n_entries = tab.shape
    width = idx.shape[1]
    return pl.pallas_call(
        _lookup_kernel,
        grid=(width // chunk,),
        in_specs=[pl.BlockSpec((rows, n_entries), lambda i: (0, 0)),
                  pl.BlockSpec((1, chunk), lambda i: (0, i)),
                  pl.BlockSpec((1, chunk), lambda i: (0, i))],
        out_specs=pl.BlockSpec((rows, chunk), lambda i: (0, i)),
        out_shape=jax.ShapeDtypeStruct((rows, width), F32),
        name="bias_lookup",
    )(tab, idx, mask)


def _t5_bucket(rel):
    nb = T5_BUCKETS // 2
    max_exact = nb // 2
    ret = (rel > 0).astype(jnp.int32) * nb
    n = jnp.abs(rel).astype(jnp.int32)
    nf = jnp.maximum(n, 1).astype(jnp.float32)
    large = max_exact + (jnp.log(nf / max_exact) / math.log(T5_MAX_DIST / max_exact)
                         * (nb - max_exact)).astype(jnp.int32)
    large = jnp.minimum(large, nb - 1)
    return ret + jnp.where(n < max_exact, n, large)


def _window_bias_table(t5_table):
    q = jnp.arange(BLOCK)[:, None]
    s = jnp.arange(3 * BLOCK)[None, :]
    rel = jnp.stack([s - q, s - BLOCK - q, s - 2 * BLOCK - q])
    idx = _t5_bucket(rel).reshape(1, -1)
    mask = (jnp.abs(rel) <= WINDOW).astype(F32).reshape(1, -1)
    tab = _lookup(t5_table.T.astype(F32), idx, mask, 3 * BLOCK * 32)
    tab = tab.reshape(A_KV_HEADS, 2, 2, 3, BLOCK, 3 * BLOCK)
    tab = tab.transpose(3, 0, 2, 1, 4, 5)
    return tab.reshape(3, A_KV_HEADS, 2, 2 * BLOCK, 3 * BLOCK)


def _na_bias_table(na_rpb):
    depth = na_rpb.shape[0]
    n_dr, n_dc = 2 * NA_ROWS - 1, 2 * NA_COLS - 1
    c = jnp.arange(GRID_W)
    col_start = jnp.clip(c - NA_COLS // 2, 0, GRID_W - NA_COLS)
    col_mask = (c[None, :] >= col_start[:, None]) & (c[None, :] < col_start[:, None] + NA_COLS)
    dc = jnp.clip(c[None, :] - c[:, None], -(NA_COLS - 1), NA_COLS - 1) + (NA_COLS - 1)
    tab = jnp.pad(na_rpb.reshape(depth * B_HEADS * n_dr, n_dc).astype(F32), ((0, 0), (0, 1)))
    out = _lookup(tab, dc.reshape(1, -1).astype(jnp.int32), col_mask.astype(F32).reshape(1, -1),
                  GRID_W * GRID_W)
    out = out.reshape(depth, B_HEADS, n_dr, GRID_W, GRID_W)
    out = jnp.pad(out, ((0, 0), (0, 0), (1, 2), (0, 0), (0, 0)))
    return jnp.concatenate([out[:, :, :-1], out[:, :, 1:]], axis=-1)


def _proj_kernel(x_ref, w_ref, hq_ref, hkv_ref):
    h = _dot(x_ref[...].astype(BF16), w_ref[...])
    hq_ref[...] = (h[:, :Q_WIDTH] * (HEAD_DIM ** -0.5)).astype(BF16)
    hkv_ref[...] = h[:, Q_WIDTH:].astype(BF16)


def _proj(x2d, w_cat, tm):
    n, d = x2d.shape
    return pl.pallas_call(
        _proj_kernel,
        grid=(n // tm,),
        in_specs=[pl.BlockSpec((tm, d), lambda i: (i, 0)),
                  pl.BlockSpec(w_cat.shape, lambda i: (0, 0))],
        out_specs=[pl.BlockSpec((tm, Q_WIDTH), lambda i: (i, 0)),
                   pl.BlockSpec((tm, KV_WIDTH), lambda i: (i, 0))],
        out_shape=[jax.ShapeDtypeStruct((n, Q_WIDTH), BF16),
                   jax.ShapeDtypeStruct((n, KV_WIDTH), BF16)],
        compiler_params=pltpu.CompilerParams(dimension_semantics=("parallel",),
                                             vmem_limit_bytes=VMEM_LIMIT),
        name="proj",
    )(x2d, w_cat)


def _proj_weight(w_in_l):
    o1 = A_WIDTH
    o2 = o1 + A_KV_WIDTH
    o3 = o2 + A_KV_WIDTH
    o4 = o3 + B_WIDTH
    o5 = o4 + B_WIDTH

    def dup(w, heads):
        w = w.reshape(w.shape[0], heads, 1, HEAD_DIM)
        return jnp.broadcast_to(w, (w.shape[0], heads, 2, HEAD_DIM)).reshape(w.shape[0], heads * 2 * HEAD_DIM)

    cols = [w_in_l[:, :o1], w_in_l[:, o3:o4],
            dup(w_in_l[:, o1:o2], A_KV_HEADS), dup(w_in_l[:, o2:o3], A_KV_HEADS),
            w_in_l[:, o4:o5], dup(w_in_l[:, o5:], B_HEADS)]
    return jnp.concatenate(cols, axis=1).astype(BF16)


def _attn_kernel(sink_ref, hq_ref, hkv_ref, x_ref, wb_ref, cb_ref, wo_ref, ga_ref, gb_ref,
                 lg_ref, lb_ref, o_ref, ya_scr, yb_scr, *, seq, tq, alpha):
    t = pl.program_id(1)
    n_blocks = seq // BLOCK
    rows = seq // GRID_W
    low_half = lax.broadcasted_iota(jnp.int32, (1, LANES), 1) < HEAD_DIM

    def split_halves(kd):
        zero = jnp.zeros_like(kd)
        return jnp.where(low_half, kd, zero), jnp.where(low_half, zero, kd)

    top_rows = lax.broadcasted_iota(jnp.int32, (2 * BLOCK, 1), 0) < BLOCK
    for j in range(tq // BLOCK):
        nb = t * (tq // BLOCK) + j
        kstart = pl.multiple_of(jnp.clip(nb - 1, 0, n_blocks - 3) * BLOCK, BLOCK)
        var = jnp.where(nb == 0, 0, jnp.where(nb == n_blocks - 1, 2, 1))
        qrows = slice(j * BLOCK, (j + 1) * BLOCK)
        for kv in range(A_KV_HEADS):
            c0 = kv * 2 * LANES
            q2 = jnp.concatenate([hq_ref[0, qrows, c0:c0 + LANES],
                                  hq_ref[0, qrows, c0 + LANES:c0 + 2 * LANES]], axis=0)
            kd = hkv_ref[0, pl.ds(kstart, 3 * BLOCK), KV_KA + kv * LANES:KV_KA + (kv + 1) * LANES]
            vd = hkv_ref[0, pl.ds(kstart, 3 * BLOCK), KV_VA + kv * LANES:KV_VA + (kv + 1) * LANES]
            outs = []
            for par, kk in enumerate(split_halves(kd)):
                s = _dot_nt(q2, kk) + wb_ref[var, kv, par]
                sk = jnp.where(top_rows, sink_ref[4 * kv + par], sink_ref[4 * kv + par + 2])
                m = jnp.maximum(jnp.max(s, axis=-1, keepdims=True), sk)
                e = jnp.exp(s - m)
                den = jnp.sum(e, axis=-1, keepdims=True) + jnp.exp(sk - m)
                p = e * (1.0 / den)
                outs.append(_dot(p.astype(BF16), vd))
            slab = jnp.where(low_half, outs[0], outs[1])
            ya_scr[qrows, c0:c0 + LANES] = slab[:BLOCK]
            ya_scr[qrows, c0 + LANES:c0 + 2 * LANES] = slab[BLOCK:]

    gq = NA_GROUP_ROWS * GRID_W
    gk = NA_KEY_ROWS * GRID_W
    for grp in range(tq // gq):
        r0 = (t * (tq // gq) + grp) * NA_GROUP_ROWS
        ws = jnp.clip(r0 - NA_ROWS // 2, 0, rows - NA_KEY_ROWS)
        sd = ws - r0
        kbase = pl.multiple_of(ws * GRID_W, GRID_W)
        shift = GRID_W.bit_length() - 1
        r = r0 + (lax.broadcasted_iota(jnp.int32, (gq, 1), 0) >> shift)
        ka = ws + (lax.broadcasted_iota(jnp.int32, (1, gk), 1) >> shift)
        rs = jnp.clip(r - NA_ROWS // 2, 0, rows - NA_ROWS)
        row_mask = jnp.where((ka >= rs) & (ka < rs + NA_ROWS), 0.0, NEG)
        grows = slice(grp * gq, (grp + 1) * gq)
        for a in range(B_HEADS // 2):
            qs = hq_ref[0, grows, A_WIDTH + a * LANES:A_WIDTH + (a + 1) * LANES]
            kd = hkv_ref[0, pl.ds(kbase, gk), KV_KB + a * LANES:KV_KB + (a + 1) * LANES]
            outs = []
            for par, kk in enumerate(split_halves(kd)):
                h = 2 * a + par
                bias_rows = []
                for rq in range(NA_GROUP_ROWS):
                    blocks = []
                    for kp in range(NA_KEY_ROWS // 2):
                        e = jnp.clip(sd + (2 * kp - rq + 8), 0, 16)
                        blocks.append(cb_ref[h, e])
                    bias_rows.append(jnp.concatenate(blocks, axis=1))
                bias = jnp.concatenate(bias_rows, axis=0)
                s = _dot_nt(qs, kk) + bias + row_mask
                m = jnp.max(s, axis=-1, keepdims=True)
                e = jnp.exp(s - m)
                p = e * (1.0 / jnp.sum(e, axis=-1, keepdims=True))
                vd = hkv_ref[0, pl.ds(kbase, gk), KV_VB + h * LANES:KV_VB + (h + 1) * LANES]
                outs.append(_dot(p.astype(BF16), vd))
            yb_scr[grows, a * LANES:(a + 1) * LANES] = jnp.where(low_half, outs[0], outs[1])

    na = _rms_norm(ya_scr[...], ga_ref[...])
    nb_ = _rms_norm(yb_scr[...], gb_ref[...])
    mix = jnp.concatenate([na, nb_], axis=-1).astype(BF16)
    y = alpha * x_ref[0] + _dot(mix, wo_ref[...])
    o_ref[0] = _layer_norm(y, lg_ref[...], lb_ref[...])


def _attn(sink, hq, hkv, x3d, wb, cb, wo, ga, gb, lg, lb, tq, alpha):
    b, seq, d = x3d.shape
    const = lambda *shape: pl.BlockSpec(shape, lambda i, j, s: (0,) * len(shape),
                                        pipeline_mode=pl.Buffered(1))
    grid_spec = pltpu.PrefetchScalarGridSpec(
        num_scalar_prefetch=1,
        grid=(b, seq // tq),
        in_specs=[pl.BlockSpec((1, tq, Q_WIDTH), lambda i, j, s: (i, j, 0)),
                  pl.BlockSpec((1, seq, KV_WIDTH), lambda i, j, s: (i, 0, 0),
                               pipeline_mode=pl.Buffered(1)),
                  pl.BlockSpec((1, tq, d), lambda i, j, s: (i, j, 0)),
                  const(*wb.shape), const(*cb.shape), const(*wo.shape),
                  const(*ga.shape), const(*gb.shape), const(*lg.shape), const(*lb.shape)],
        out_specs=pl.BlockSpec((1, tq, d), lambda i, j, s: (i, j, 0)),
        scratch_shapes=[pltpu.VMEM((tq, A_WIDTH), F32), pltpu.VMEM((tq, B_WIDTH), F32)],
    )
    return pl.pallas_call(
        functools.partial(_attn_kernel, seq=seq, tq=tq, alpha=alpha),
        grid_spec=grid_spec,
        out_shape=jax.ShapeDtypeStruct((b, seq, d), F32),
        compiler_params=pltpu.CompilerParams(dimension_semantics=("parallel", "arbitrary"),
                                             vmem_limit_bytes=VMEM_LIMIT),
        name="attn",
    )(sink, hq, hkv, x3d, wb, cb, wo, ga, gb, lg, lb)


def _tree(op, xs):
    xs = list(xs)
    while len(xs) > 1:
        xs = [op(xs[i], xs[i + 1]) for i in range(0, len(xs) - 1, 2)] + ([xs[-1]] if len(xs) % 2 else [])
    return xs[0]


def _route_kernel(x_ref, wq_ref, kbd_ref, am_ref, cnt_ref, e1_ref, rk_ref,
                  q_scr, s_scr, w_scr, rank_scr, vals_scr, e_scr, *, tb):
    k = PEER_TOPK
    nk = PEER_N_KEYS
    nh = PEER_HEADS
    half = nh * (PEER_QDIM // 2)
    xt = x_ref[...].T.astype(BF16)
    q_scr[...] = _dot(wq_ref[...], xt).astype(BF16)
    for p in range(2):
        s_scr[p] = _dot(kbd_ref[p], q_scr[p * half:(p + 1) * half, :])

    def slab(kk):
        return slice(kk * nh, (kk + 1) * nh)

    for lb in range(tb // LANES):
        lanes = slice(lb * LANES, (lb + 1) * LANES)

        for p in range(2):
            w_scr[...] = s_scr[p, :, lanes]
            rank_scr[p] = jnp.full((nk * nh, LANES), 99.0, F32)

            def round_body(a, _, p=p):
                vals = [w_scr[slab(kk), :] for kk in range(nk)]
                m = _tree(jnp.maximum, vals)
                idx = _tree(jnp.minimum, [jnp.where(v == m, float(kk), float(nk))
                                          for kk, v in enumerate(vals)])
                vals_scr[p, pl.ds(pl.multiple_of(a * nh, nh), nh), :] = m
                af = lax.convert_element_type(a, F32)
                for kk in range(nk):
                    hit = idx == float(kk)
                    w_scr[slab(kk), :] = jnp.where(hit, -jnp.inf, vals[kk])
                    rank_scr[p, slab(kk), :] = jnp.where(hit, af, rank_scr[p, slab(kk), :])
                return 0

            lax.fori_loop(0, k, round_body, 0)

        v0 = vals_scr[0].reshape(k, nh, LANES)
        v1 = vals_scr[1].reshape(k, nh, LANES)
        iota_a = lax.broadcasted_iota(jnp.int32, (k, nh, LANES), 0).astype(F32)

        def merge_body(_, carry):
            cnt, front = carry
            m = jnp.max(front, axis=0, keepdims=True)
            asel = jnp.min(jnp.where(front == m, iota_a, 99.0), axis=0, keepdims=True)
            hit = iota_a == asel
            cnt = cnt + hit.astype(F32)
            csel = jnp.sum(jnp.where(hit, cnt, 0.0), axis=0, keepdims=True)
            nxt = jnp.sum(jnp.where(iota_a == csel, v1, 0.0), axis=0, keepdims=True)
            nxt = jnp.where(csel >= k, -jnp.inf, nxt)
            return cnt, jnp.where(hit, v0 + nxt, front)

        cnt, _ = lax.fori_loop(0, k, merge_body, (jnp.zeros((k, nh, LANES), F32), v0 + v1[0:1]))

        e0 = jnp.exp(v0 - v0[0:1])
        e1 = jnp.exp(v1 - v1[0:1])
        pref = jnp.zeros((k, nh, LANES), F32)
        for b in range(k):
            pref = pref + jnp.where(cnt > b, e1[b:b + 1], 0.0)
        inv_z = 1.0 / jnp.sum(e0 * pref, axis=0)

        for kk in range(nk):
            rank0 = rank_scr[0, slab(kk), :]
            gate0 = jnp.exp(s_scr[0, slab(kk), lanes] - v0[0]) * inv_z
            am_ref[kk, :, lanes] = jnp.where(rank0 < k, gate0, 0.0)
            cnt_ref[kk, :, lanes] = _tree(jnp.add, [jnp.where(rank0 == a, cnt[a], 0.0) for a in range(k)])
            e_scr[slab(kk), :] = jnp.exp(s_scr[1, slab(kk), lanes] - v1[0])
        for h in range(nh):
            e1_ref[h, :, lanes] = e_scr[pl.ds(h, nk, stride=nh), :].astype(BF16)
            rk_ref[h, :, lanes] = rank_scr[1, pl.ds(h, nk, stride=nh), :].astype(BF16)


def _route(x1, wq_t, kbd, tb):
    n, d = x1.shape
    nk = PEER_N_KEYS
    nh = PEER_HEADS
    by_key = jax.ShapeDtypeStruct((nk, nh, n), F32)
    by_head = jax.ShapeDtypeStruct((nh, nk, n), BF16)
    kspec = pl.BlockSpec((nk, nh, tb), lambda i: (0, 0, i))
    hspec = pl.BlockSpec((nh, nk, tb), lambda i: (0, 0, i))
    return pl.pallas_call(
        functools.partial(_route_kernel, tb=tb),
        grid=(n // tb,),
        in_specs=[pl.BlockSpec((tb, d), lambda i: (i, 0)),
                  pl.BlockSpec(wq_t.shape, lambda i: (0, 0)),
                  pl.BlockSpec(kbd.shape, lambda i: (0, 0, 0))],
        out_specs=[kspec, kspec, hspec, hspec],
        out_shape=[by_key, by_key, by_head, by_head],
        scratch_shapes=[pltpu.VMEM((2 * nh * nk, tb), BF16),
                        pltpu.VMEM((2, nk * nh, tb), F32),
                        pltpu.VMEM((nk * nh, LANES), F32),
                        pltpu.VMEM((2, nk * nh, LANES), F32),
                        pltpu.VMEM((2, PEER_TOPK * nh, LANES), F32),
                        pltpu.VMEM((nk * nh, LANES), F32)],
        compiler_params=pltpu.CompilerParams(dimension_semantics=("parallel",),
                                             vmem_limit_bytes=VMEM_LIMIT),
        name="peer_route",
    )(x1, wq_t, kbd)


def _route_weights(peer_wq_l, peer_keys_l):
    nh, nk, c = PEER_HEADS, PEER_N_KEYS, PEER_QDIM // 2
    d = peer_wq_l.shape[0]
    wq_t = peer_wq_l.T.reshape(nh, 2, c, d).transpose(1, 0, 2, 3).reshape(2 * nh * c, d)
    kbd = jnp.einsum('hpkc,hg->pkhgc', peer_keys_l, jnp.eye(nh, dtype=peer_keys_l.dtype))
    return wq_t.astype(BF16), kbd.reshape(2, nk * nh, nh * c).astype(BF16)


def _dense_kernel(xa_ref, xb_ref, am_ref, cnt_ref, e1_ref, rk_ref, u_ref, vt_ref, lg_ref, lb_ref,
                  o_ref, xt_scr, h0_scr, h1_scr, a0_scr, a1_scr, acc_scr, *, tb, ec, nch, alpha):
    s = pl.program_id(0)
    nk = PEER_N_KEYS
    phase = s % nch
    h_bufs = (h0_scr, h1_scr)
    a_bufs = (a0_scr, a1_scr)

    @pl.when(s == 0)
    def _():
        for buf in h_bufs + a_bufs + (acc_scr,):
            buf[...] = jnp.zeros_like(buf)

    @pl.when(phase == 0)
    def _():
        xt_scr[...] = xa_ref[...].T.astype(BF16)

    @pl.when(phase == 2 % nch)
    def _():
        acc_scr[...] = jnp.zeros_like(acc_scr)

    def stage(cur, part):
        h_cur, h_prv = h_bufs[cur], h_bufs[1 - cur]
        a_cur, a_prv = a_bufs[cur], a_bufs[1 - cur]
        tw = tb // DENSE_PARTS
        toks = slice(part * tw, (part + 1) * tw)
        vt = pltpu.bitcast(vt_ref[...], BF16)
        u = pltpu.bitcast(u_ref[...], BF16)
        acc_scr[:, toks] += _dot(vt, a_cur[:, toks])
        h_cur[:, toks] = _dot(u, xt_scr[:, toks])
        for ii in range(ec // nk):
            rows = slice(ii * nk, (ii + 1) * nk)
            for lb in range(part * tw // LANES, (part + 1) * tw // LANES):
                lanes = slice(lb * LANES, (lb + 1) * LANES)
                packed = (nk // BF16_ROWS, BF16_ROWS, LANES)
                gate = jnp.zeros(packed, BF16)
                for h in range(PEER_HEADS):
                    am = jnp.broadcast_to(am_ref[ii, h:h + 1, lanes], packed[1:]).astype(BF16)
                    cnt = jnp.broadcast_to(cnt_ref[ii, h:h + 1, lanes], packed[1:]).astype(BF16)
                    rk = rk_ref[h, :, lanes].reshape(packed)
                    e1 = e1_ref[h, :, lanes].reshape(packed)
                    gate = gate + jnp.where(cnt[None] - rk > 0, am[None] * e1, jnp.zeros((), BF16))
                hv = h_prv[rows, lanes]
                half = 0.5 * hv
                gelu = half + half * lax.erf(hv * (0.5 ** 0.5))
                a_prv[rows, lanes] = (gelu.astype(BF16).reshape(packed) * gate).reshape(nk, LANES)

    for par in range(2):
        for part in range(DENSE_PARTS):
            pl.when((s + 2 * part) % 2 == par)(functools.partial(stage, par, part))

    @pl.when((phase == 1 % nch) & (s > 1))
    def _():
        y = alpha * xb_ref[...] + acc_scr[...].T
        o_ref[...] = _layer_norm(y, lg_ref[...], lb_ref[...])


def _pack_rows(w):
    r, c = w.shape
    pairs = w.astype(BF16).reshape(r // 2, 2, c).transpose(0, 2, 1)
    return lax.bitcast_convert_type(pairs, jnp.uint32)


def _dense(x1, am, cnt, e1, rk, u, vt, lg, lb, tb, ec, alpha):
    n, d = x1.shape
    n_exp = vt.shape[1]
    nk = PEER_N_KEYS
    nch = n_exp // ec
    assert nch % 2 == 0 and nch >= 4
    total = (n // tb) * nch

    def m1(s):
        return jnp.minimum(s, total - 1)

    def gt(s):
        return jnp.clip(s - 1, 0, total - 1)

    def m2(s):
        return jnp.clip(s - 2, 0, total - 1)

    rspec = pl.BlockSpec((PEER_HEADS, nk, tb), lambda s: (0, 0, gt(s) // nch))
    cspec = pl.BlockSpec((ec // nk, PEER_HEADS, tb), lambda s: (gt(s) % nch, 0, gt(s) // nch))
    return pl.pallas_call(
        functools.partial(_dense_kernel, tb=tb, ec=ec, nch=nch, alpha=alpha),
        grid=(total + 2,),
        in_specs=[pl.BlockSpec((tb, d), lambda s: (m1(s) // nch, 0)),
                  pl.BlockSpec((tb, d), lambda s: (m2(s) // nch, 0)),
                  cspec, cspec, rspec, rspec,
                  pl.BlockSpec((ec // 2, d), lambda s: (m1(s) % nch, 0)),
                  pl.BlockSpec((d // 2, ec), lambda s: (0, m2(s) % nch)),
                  pl.BlockSpec(lg.shape, lambda s: (0, 0)),
                  pl.BlockSpec(lb.shape, lambda s: (0, 0))],
        out_specs=pl.BlockSpec((tb, d), lambda s: (m2(s) // nch, 0)),
        out_shape=jax.ShapeDtypeStruct((n, d), F32),
        scratch_shapes=[pltpu.VMEM((d, tb), BF16),
                        pltpu.VMEM((ec, tb), F32), pltpu.VMEM((ec, tb), F32),
                        pltpu.VMEM((ec, tb), BF16), pltpu.VMEM((ec, tb), BF16),
                        pltpu.VMEM((d, tb), F32)],
        compiler_params=pltpu.CompilerParams(dimension_semantics=("arbitrary",),
                                             vmem_limit_bytes=VMEM_LIMIT),
        name="peer_dense",
    )(x1, x1, am, cnt, e1, rk, u, vt, lg, lb)


def kernel(x, w_in, w_o, attn_sink, na_rpb, t5_table, gnorm_a, gnorm_b, ln1_g, ln1_b, ln2_g, ln2_b,
           peer_wq, peer_keys, peer_u, peer_v):
    b, seq, d = x.shape
    depth = w_in.shape[0]
    n = b * seq
    alpha = (2 * depth) ** 0.25
    tq = min(512, seq)
    tm = min(512, n)
    tb_route = min(256, n)
    tb_dense = min(512, n)
    ec = 2048
    assert seq % tq == 0 and seq // GRID_W >= NA_KEY_ROWS and seq // BLOCK >= 3

    wb = _window_bias_table(t5_table)
    cb_all = _na_bias_table(na_rpb)
    row = lambda v: v.reshape(1, -1)

    for l in range(depth):
        hq, hkv = _proj(x.reshape(n, d), _proj_weight(w_in[l]), tm)
        x1 = _attn(attn_sink[l], hq.reshape(b, seq, Q_WIDTH), hkv.reshape(b, seq, KV_WIDTH), x,
                   wb, cb_all[l], w_o[l].astype(BF16), row(gnorm_a[l]), row(gnorm_b[l]),
                   row(ln1_g[l]), row(ln1_b[l]), tq, alpha)
        x1 = x1.reshape(n, d)
        wq_t, kbd = _route_weights(peer_wq[l], peer_keys[l])
        am, cnt, e1, rk = _route(x1, wq_t, kbd, tb_route)
        x = _dense(x1, am, cnt, e1, rk, _pack_rows(peer_u[l]), _pack_rows(peer_v[l].T),
                   row(ln2_g[l]), row(ln2_b[l]), tb_dense, ec, alpha).reshape(b, seq, d)
    return x
```

```python
import functools
import math

import numpy as np
import jax
import jax.numpy as jnp
from jax import lax
from jax.experimental import pallas as pl
from jax.experimental.pallas import tpu as pltpu

F32 = jnp.float32
BF16 = jnp.bfloat16

HEAD_DIM = 64
A_HEADS = 8
A_KV_HEADS = 2
B_HEADS = 8
A_WIDTH = A_HEADS * HEAD_DIM
A_KV_WIDTH = A_KV_HEADS * HEAD_DIM
B_WIDTH = B_HEADS * HEAD_DIM
WINDOW = 128
BLOCK = 128
GRID_W = 64
NA_ROWS = 8
NA_COLS = 16
T5_BUCKETS = 32
T5_MAX_DIST = 128
PEER_HEADS = 8
PEER_N_KEYS = 128
PEER_TOPK = 16
PEER_QDIM = 256
LN_EPS = 1e-5
NEG = -1e30

LANES = 128
BF16_ROWS = 16
DENSE_PARTS = 1
NA_GROUP_ROWS = 4
NA_KEY_ROWS = 12
Q_WIDTH = A_WIDTH + B_WIDTH
KV_WIDTH = 2 * A_KV_WIDTH + 2 * A_KV_WIDTH + B_WIDTH + 2 * B_WIDTH
KV_KA, KV_VA, KV_KB, KV_VB = 0, 2 * A_KV_WIDTH, 4 * A_KV_WIDTH, 4 * A_KV_WIDTH + B_WIDTH

VMEM_LIMIT = 56 * 1024 * 1024


def _dot(a, b):
    return jnp.dot(a, b, preferred_element_type=F32)


def _dot_nt(a, b):
    return lax.dot_general(a, b, (((1,), (1,)), ((), ())), preferred_element_type=F32)


def _layer_norm(y, g, b):
    mu = jnp.mean(y, axis=-1, keepdims=True)
    d = y - mu
    var = jnp.mean(d * d, axis=-1, keepdims=True)
    return d * lax.rsqrt(var + LN_EPS) * g + b


def _rms_norm(y, g):
    return y * lax.rsqrt(jnp.mean(y * y, axis=-1, keepdims=True) + LN_EPS) * g


def _lookup_kernel(tab_ref, idx_ref, mask_ref, o_ref):
    n_entries = tab_ref.shape[1]
    width = idx_ref.shape[1]
    onehot = (lax.broadcasted_iota(jnp.int32, (n_entries, width), 0) == idx_ref[...]).astype(F32)
    val = jnp.dot(tab_ref[...], onehot, preferred_element_type=F32, precision=lax.Precision.HIGHEST)
    o_ref[...] = jnp.where(mask_ref[...] > 0, val, NEG)


def _lookup(tab, idx, mask, chunk):
    rows, n_entries = tab.shape
    width = idx.shape[1]
    return pl.pallas_call(
        _lookup_kernel,
        grid=(width // chunk,),
        in_specs=[pl.BlockSpec((rows, n_entries), lambda i: (0, 0)),
                  pl.BlockSpec((1, chunk), lambda i: (0, i)),
                  pl.BlockSpec((1, chunk), lambda i: (0, i))],
        out_specs=pl.BlockSpec((rows, chunk), lambda i: (0, i)),
        out_shape=jax.ShapeDtypeStruct((rows, width), F32),
        name="bias_lookup",
    )(tab, idx, mask)


def _t5_bucket(rel):
    nb = T5_BUCKETS // 2
    max_exact = nb // 2
    ret = (rel > 0).astype(jnp.int32) * nb
    n = jnp.abs(rel).astype(jnp.int32)
    nf = jnp.maximum(n, 1).astype(jnp.float32)
    large = max_exact + (jnp.log(nf / max_exact) / math.log(T5_MAX_DIST / max_exact)
                         * (nb - max_exact)).astype(jnp.int32)
    large = jnp.minimum(large, nb - 1)
    return ret + jnp.where(n < max_exact, n, large)


def _window_bias_table(t5_table):
    q = jnp.arange(BLOCK)[:, None]
    s = jnp.arange(3 * BLOCK)[None, :]
    rel = jnp.stack([s - q, s - BLOCK - q, s - 2 * BLOCK - q])
    idx = _t5_bucket(rel).reshape(1, -1)
    mask = (jnp.abs(rel) <= WINDOW).astype(F32).reshape(1, -1)
    tab = _lookup(t5_table.T.astype(F32), idx, mask, 3 * BLOCK * 32)
    tab = tab.reshape(A_KV_HEADS, 2, 2, 3, BLOCK, 3 * BLOCK)
    tab = tab.transpose(3, 0, 2, 1, 4, 5)
    return tab.reshape(3, A_KV_HEADS, 2, 2 * BLOCK, 3 * BLOCK)


def _na_bias_table(na_rpb):
    depth = na_rpb.shape[0]
    n_dr, n_dc = 2 * NA_ROWS - 1, 2 * NA_COLS - 1
    c = jnp.arange(GRID_W)
    col_start = jnp.clip(c - NA_COLS // 2, 0, GRID_W - NA_COLS)
    col_mask = (c[None, :] >= col_start[:, None]) & (c[None, :] < col_start[:, None] + NA_COLS)
    dc = jnp.clip(c[None, :] - c[:, None], -(NA_COLS - 1), NA_COLS - 1) + (NA_COLS - 1)
    tab = jnp.pad(na_rpb.reshape(depth * B_HEADS * n_dr, n_dc).astype(F32), ((0, 0), (0, 1)))
    out = _lookup(tab, dc.reshape(1, -1).astype(jnp.int32), col_mask.astype(F32).reshape(1, -1),
                  GRID_W * GRID_W)
    out = out.reshape(depth, B_HEADS, n_dr, GRID_W, GRID_W)
    out = jnp.pad(out, ((0, 0), (0, 0), (1, 2), (0, 0), (0, 0)))
    return jnp.concatenate([out[:, :, :-1], out[:, :, 1:]], axis=-1)


def _proj_kernel(x_ref, w_ref, hq_ref, hkv_ref):
    h = _dot(x_ref[...].astype(BF16), w_ref[...])
    hq_ref[...] = (h[:, :Q_WIDTH] * (HEAD_DIM ** -0.5)).astype(BF16)
    hkv_ref[...] = h[:, Q_WIDTH:].astype(BF16)


def _proj(x2d, w_cat, tm):
    n, d = x2d.shape
    return pl.pallas_call(
        _proj_kernel,
        grid=(n // tm,),
        in_specs=[pl.BlockSpec((tm, d), lambda i: (i, 0)),
                  pl.BlockSpec(w_cat.shape, lambda i: (0, 0))],
        out_specs=[pl.BlockSpec((tm, Q_WIDTH), lambda i: (i, 0)),
                   pl.BlockSpec((tm, KV_WIDTH), lambda i: (i, 0))],
        out_shape=[jax.ShapeDtypeStruct((n, Q_WIDTH), BF16),
                   jax.ShapeDtypeStruct((n, KV_WIDTH), BF16)],
        compiler_params=pltpu.CompilerParams(dimension_semantics=("parallel",),
                                             vmem_limit_bytes=VMEM_LIMIT),
        name="proj",
    )(x2d, w_cat)


def _proj_weight(w_in_l):
    o1 = A_WIDTH
    o2 = o1 + A_KV_WIDTH
    o3 = o2 + A_KV_WIDTH
    o4 = o3 + B_WIDTH
    o5 = o4 + B_WIDTH

    def dup(w, heads):
        w = w.reshape(w.shape[0], heads, 1, HEAD_DIM)
        return jnp.broadcast_to(w, (w.shape[0], heads, 2, HEAD_DIM)).reshape(w.shape[0], heads * 2 * HEAD_DIM)

    cols = [w_in_l[:, :o1], w_in_l[:, o3:o4],
            dup(w_in_l[:, o1:o2], A_KV_HEADS), dup(w_in_l[:, o2:o3], A_KV_HEADS),
            w_in_l[:, o4:o5], dup(w_in_l[:, o5:], B_HEADS)]
    return jnp.concatenate(cols, axis=1).astype(BF16)


def _attn_kernel(sink_ref, hq_ref, hkv_ref, x_ref, wb_ref, cb_ref, wo_ref, ga_ref, gb_ref,
                 lg_ref, lb_ref, o_ref, ya_scr, yb_scr, *, seq, tq, alpha):
    t = pl.program_id(1)
    n_blocks = seq // BLOCK
    rows = seq // GRID_W
    low_half = lax.broadcasted_iota(jnp.int32, (1, LANES), 1) < HEAD_DIM

    def split_halves(kd):
        zero = jnp.zeros_like(kd)
        return jnp.where(low_half, kd, zero), jnp.where(low_half, zero, kd)

    top_rows = lax.broadcasted_iota(jnp.int32, (2 * BLOCK, 1), 0) < BLOCK
    for j in range(tq // BLOCK):
        nb = t * (tq // BLOCK) + j
        kstart = pl.multiple_of(jnp.clip(nb - 1, 0, n_blocks - 3) * BLOCK, BLOCK)
        var = jnp.where(nb == 0, 0, jnp.where(nb == n_blocks - 1, 2, 1))
        qrows = slice(j * BLOCK, (j + 1) * BLOCK)
        for kv in range(A_KV_HEADS):
            c0 = kv * 2 * LANES
            q2 = jnp.concatenate([hq_ref[0, qrows, c0:c0 + LANES],
                                  hq_ref[0, qrows, c0 + LANES:c0 + 2 * LANES]], axis=0)
            kd = hkv_ref[0, pl.ds(kstart, 3 * BLOCK), KV_KA + kv * LANES:KV_KA + (kv + 1) * LANES]
            vd = hkv_ref[0, pl.ds(kstart, 3 * BLOCK), KV_VA + kv * LANES:KV_VA + (kv + 1) * LANES]
            outs = []
            for par, kk in enumerate(split_halves(kd)):
                s = _dot_nt(q2, kk) + wb_ref[var, kv, par]
                sk = jnp.where(top_rows, sink_ref[4 * kv + par], sink_ref[4 * kv + par + 2])
                m = jnp.maximum(jnp.max(s, axis=-1, keepdims=True), sk)
                e = jnp.exp(s - m)
                den = jnp.sum(e, axis=-1, keepdims=True) + jnp.exp(sk - m)
                p = e * (1.0 / den)
                outs.append(_dot(p.astype(BF16), vd))
            slab = jnp.where(low_half, outs[0], outs[1])
            ya_scr[qrows, c0:c0 + LANES] = slab[:BLOCK]
            ya_scr[qrows, c0 + LANES:c0 + 2 * LANES] = slab[BLOCK:]

    gq = NA_GROUP_ROWS * GRID_W
    gk = NA_KEY_ROWS * GRID_W
    for grp in range(tq // gq):
        r0 = (t * (tq // gq) + grp) * NA_GROUP_ROWS
        ws = jnp.clip(r0 - NA_ROWS // 2, 0, rows - NA_KEY_ROWS)
        sd = ws - r0
        kbase = pl.multiple_of(ws * GRID_W, GRID_W)
        shift = GRID_W.bit_length() - 1
        r = r0 + (lax.broadcasted_iota(jnp.int32, (gq, 1), 0) >> shift)
        ka = ws + (lax.broadcasted_iota(jnp.int32, (1, gk), 1) >> shift)
        rs = jnp.clip(r - NA_ROWS // 2, 0, rows - NA_ROWS)
        row_mask = jnp.where((ka >= rs) & (ka < rs + NA_ROWS), 0.0, NEG)
        grows = slice(grp * gq, (grp + 1) * gq)
        for a in range(B_HEADS // 2):
            qs = hq_ref[0, grows, A_WIDTH + a * LANES:A_WIDTH + (a + 1) * LANES]
            kd = hkv_ref[0, pl.ds(kbase, gk), KV_KB + a * LANES:KV_KB + (a + 1) * LANES]
            outs = []
            for par, kk in enumerate(split_halves(kd)):
                h = 2 * a + par
                bias_rows = []
                for rq in range(NA_GROUP_ROWS):
                    blocks = []
                    for kp in range(NA_KEY_ROWS // 2):
                        e = jnp.clip(sd + (2 * kp - rq + 8), 0, 16)
                        blocks.append(cb_ref[h, e])
                    bias_rows.append(jnp.concatenate(blocks, axis=1))
                bias = jnp.concatenate(bias_rows, axis=0)
                s = _dot_nt(qs, kk) + bias + row_mask
                m = jnp.max(s, axis=-1, keepdims=True)
                e = jnp.exp(s - m)
                p = e * (1.0 / jnp.sum(e, axis=-1, keepdims=True))
                vd = hkv_ref[0, pl.ds(kbase, gk), KV_VB + h * LANES:KV_VB + (h + 1) * LANES]
                outs.append(_dot(p.astype(BF16), vd))
            yb_scr[grows, a * LANES:(a + 1) * LANES] = jnp.where(low_half, outs[0], outs[1])

    na = _rms_norm(ya_scr[...], ga_ref[...])
    nb_ = _rms_norm(yb_scr[...], gb_ref[...])
    mix = jnp.concatenate([na, nb_], axis=-1).astype(BF16)
    y = alpha * x_ref[0] + _dot(mix, wo_ref[...])
    o_ref[0] = _layer_norm(y, lg_ref[...], lb_ref[...])


def _attn(sink, hq, hkv, x3d, wb, cb, wo, ga, gb, lg, lb, tq, alpha):
    b, seq, d = x3d.shape
    const = lambda *shape: pl.BlockSpec(shape, lambda i, j, s: (0,) * len(shape),
                                        pipeline_mode=pl.Buffered(1))
    grid_spec = pltpu.PrefetchScalarGridSpec(
        num_scalar_prefetch=1,
        grid=(b, seq // tq),
        in_specs=[pl.BlockSpec((1, tq, Q_WIDTH), lambda i, j, s: (i, j, 0)),
                  pl.BlockSpec((1, seq, KV_WIDTH), lambda i, j, s: (i, 0, 0),
                               pipeline_mode=pl.Buffered(1)),
                  pl.BlockSpec((1, tq, d), lambda i, j, s: (i, j, 0)),
                  const(*wb.shape), const(*cb.shape), const(*wo.shape),
                  const(*ga.shape), const(*gb.shape), const(*lg.shape), const(*lb.shape)],
        out_specs=pl.BlockSpec((1, tq, d), lambda i, j, s: (i, j, 0)),
        scratch_shapes=[pltpu.VMEM((tq, A_WIDTH), F32), pltpu.VMEM((tq, B_WIDTH), F32)],
    )
    return pl.pallas_call(
        functools.partial(_attn_kernel, seq=seq, tq=tq, alpha=alpha),
        grid_spec=grid_spec,
        out_shape=jax.ShapeDtypeStruct((b, seq, d), F32),
        compiler_params=pltpu.CompilerParams(dimension_semantics=("parallel", "arbitrary"),
                                             vmem_limit_bytes=VMEM_LIMIT),
        name="attn",
    )(sink, hq, hkv, x3d, wb, cb, wo, ga, gb, lg, lb)


def _tree(op, xs):
    xs = list(xs)
    while len(xs) > 1:
        xs = [op(xs[i], xs[i + 1]) for i in range(0, len(xs) - 1, 2)] + ([xs[-1]] if len(xs) % 2 else [])
    return xs[0]


def _route_kernel(x_ref, wq_ref, kbd_ref, am_ref, cnt_ref, e1_ref, rk_ref,
                  q_scr, s_scr, w_scr, rank_scr, vals_scr, e_scr, *, tb):
    k = PEER_TOPK
    nk = PEER_N_KEYS
    nh = PEER_HEADS
    half = nh * (PEER_QDIM // 2)
    xt = x_ref[...].T.astype(BF16)
    q_scr[...] = _dot(wq_ref[...], xt).astype(BF16)
    for p in range(2):
        s_scr[p] = _dot(kbd_ref[p], q_scr[p * half:(p + 1) * half, :])

    def slab(kk):
        return slice(kk * nh, (kk + 1) * nh)

    for lb in range(tb // LANES):
        lanes = slice(lb * LANES, (lb + 1) * LANES)

        for p in range(2):
            w_scr[...] = s_scr[p, :, lanes]
            rank_scr[p] = jnp.full((nk * nh, LANES), 99.0, F32)

            def round_body(a, _, p=p):
                vals = [w_scr[slab(kk), :] for kk in range(nk)]
                m = _tree(jnp.maximum, vals)
                idx = _tree(jnp.minimum, [jnp.where(v == m, float(kk), float(nk))
                                          for kk, v in enumerate(vals)])
                vals_scr[p, pl.ds(pl.multiple_of(a * nh, nh), nh), :] = m
                af = lax.convert_element_type(a, F32)
                for kk in range(nk):
                    hit = idx == float(kk)
                    w_scr[slab(kk), :] = jnp.where(hit, -jnp.inf, vals[kk])
                    rank_scr[p, slab(kk), :] = jnp.where(hit, af, rank_scr[p, slab(kk), :])
                return 0

            lax.fori_loop(0, k, round_body, 0)

        v0 = vals_scr[0].reshape(k, nh, LANES)
        v1 = vals_scr[1].reshape(k, nh, LANES)
        iota_a = lax.broadcasted_iota(jnp.int32, (k, nh, LANES), 0).astype(F32)

        def merge_body(_, carry):
            cnt, front = carry
            m = jnp.max(front, axis=0, keepdims=True)
            asel = jnp.min(jnp.where(front == m, iota_a, 99.0), axis=0, keepdims=True)
            hit = iota_a == asel
            cnt = cnt + hit.astype(F32)
            csel = jnp.sum(jnp.where(hit, cnt, 0.0), axis=0, keepdims=True)
            nxt = jnp.sum(jnp.where(iota_a == csel, v1, 0.0), axis=0, keepdims=True)
            nxt = jnp.where(csel >= k, -jnp.inf, nxt)
            return cnt, jnp.where(hit, v0 + nxt, front)

        cnt, _ = lax.fori_loop(0, k, merge_body, (jnp.zeros((k, nh, LANES), F32), v0 + v1[0:1]))

        e0 = jnp.exp(v0 - v0[0:1])
        e1 = jnp.exp(v1 - v1[0:1])
        pref = jnp.zeros((k, nh, LANES), F32)
        for b in range(k):
            pref = pref + jnp.where(cnt > b, e1[b:b + 1], 0.0)
        inv_z = 1.0 / jnp.sum(e0 * pref, axis=0)

        for kk in range(nk):
            rank0 = rank_scr[0, slab(kk), :]
            gate0 = jnp.exp(s_scr[0, slab(kk), lanes] - v0[0]) * inv_z
            am_ref[kk, :, lanes] = jnp.where(rank0 < k, gate0, 0.0)
            cnt_ref[kk, :, lanes] = _tree(jnp.add, [jnp.where(rank0 == a, cnt[a], 0.0) for a in range(k)])
            e_scr[slab(kk), :] = jnp.exp(s_scr[1, slab(kk), lanes] - v1[0])
        for h in range(nh):
            e1_ref[h, :, lanes] = pltpu.bitcast(
                e_scr[pl.ds(h, nk, stride=nh), :].astype(BF16), jnp.uint32)
            rk_ref[h, :, lanes] = pltpu.bitcast(
                rank_scr[1, pl.ds(h, nk, stride=nh), :].astype(BF16), jnp.uint32)


def _route(x1, wq_t, kbd, tb):
    n, d = x1.shape
    nk = PEER_N_KEYS
    nh = PEER_HEADS
    by_key = jax.ShapeDtypeStruct((nk, nh, n), F32)
    by_head = jax.ShapeDtypeStruct((nh, nk // 2, n), jnp.uint32)
    kspec = pl.BlockSpec((nk, nh, tb), lambda i: (0, 0, i))
    hspec = pl.BlockSpec((nh, nk // 2, tb), lambda i: (0, 0, i))
    return pl.pallas_call(
        functools.partial(_route_kernel, tb=tb),
        grid=(n // tb,),
        in_specs=[pl.BlockSpec((tb, d), lambda i: (i, 0)),
                  pl.BlockSpec(wq_t.shape, lambda i: (0, 0)),
                  pl.BlockSpec(kbd.shape, lambda i: (0, 0, 0))],
        out_specs=[kspec, kspec, hspec, hspec],
        out_shape=[by_key, by_key, by_head, by_head],
        scratch_shapes=[pltpu.VMEM((2 * nh * nk, tb), BF16),
                        pltpu.VMEM((2, nk * nh, tb), F32),
                        pltpu.VMEM((nk * nh, LANES), F32),
                        pltpu.VMEM((2, nk * nh, LANES), F32),
                        pltpu.VMEM((2, PEER_TOPK * nh, LANES), F32),
                        pltpu.VMEM((nk * nh, LANES), F32)],
        compiler_params=pltpu.CompilerParams(dimension_semantics=("parallel",),
                                             vmem_limit_bytes=VMEM_LIMIT),
        name="peer_route",
    )(x1, wq_t, kbd)


def _route_weights(peer_wq_l, peer_keys_l):
    nh, nk, c = PEER_HEADS, PEER_N_KEYS, PEER_QDIM // 2
    d = peer_wq_l.shape[0]
    wq_t = peer_wq_l.T.reshape(nh, 2, c, d).transpose(1, 0, 2, 3).reshape(2 * nh * c, d)
    kbd = jnp.einsum('hpkc,hg->pkhgc', peer_keys_l, jnp.eye(nh, dtype=peer_keys_l.dtype))
    return wq_t.astype(BF16), kbd.reshape(2, nk * nh, nh * c).astype(BF16)


def _dense_kernel(xa_ref, xb_ref, am_ref, cnt_ref, e1_ref, rk_ref, u_ref, vt_ref, lg_ref, lb_ref,
                  o_ref, xt_scr, h0_scr, h1_scr, a0_scr, a1_scr, acc_scr, *, tb, ec, nch, alpha):
    s = pl.program_id(0)
    nk = PEER_N_KEYS
    phase = s % nch
    h_bufs = (h0_scr, h1_scr)
    a_bufs = (a0_scr, a1_scr)

    @pl.when(s == 0)
    def _():
        for buf in h_bufs + a_bufs + (acc_scr,):
            buf[...] = jnp.zeros_like(buf)

    @pl.when(phase == 0)
    def _():
        xt_scr[...] = xa_ref[...].T.astype(BF16)

    @pl.when(phase == 2 % nch)
    def _():
        acc_scr[...] = jnp.zeros_like(acc_scr)

    def stage(cur, part):
        h_cur, h_prv = h_bufs[cur], h_bufs[1 - cur]
        a_cur, a_prv = a_bufs[cur], a_bufs[1 - cur]
        tw = tb // DENSE_PARTS
        toks = slice(part * tw, (part + 1) * tw)
        acc_scr[:, toks] += _dot(vt_ref[...], a_cur[:, toks])
        h_cur[:, toks] = _dot(u_ref[...], xt_scr[:, toks])
        for ii in range(ec // nk):
            rows = slice(ii * nk, (ii + 1) * nk)
            for lb in range(part * tw // LANES, (part + 1) * tw // LANES):
                lanes = slice(lb * LANES, (lb + 1) * LANES)
                packed = (nk // BF16_ROWS, BF16_ROWS, LANES)
                gate = jnp.zeros(packed, BF16)
                for h in range(PEER_HEADS):
                    am = jnp.broadcast_to(am_ref[ii, h:h + 1, lanes], packed[1:]).astype(BF16)
                    cnt = jnp.broadcast_to(cnt_ref[ii, h:h + 1, lanes], packed[1:]).astype(BF16)
                    rk = pltpu.bitcast(rk_ref[h, :, lanes], BF16).reshape(packed)
                    e1 = pltpu.bitcast(e1_ref[h, :, lanes], BF16).reshape(packed)
                    gate = gate + jnp.where(cnt[None] - rk > 0, am[None] * e1, jnp.zeros((), BF16))
                hv = h_prv[rows, lanes]
                half = 0.5 * hv
                gelu = half + half * lax.erf(hv * (0.5 ** 0.5))
                a_prv[rows, lanes] = (gelu.astype(BF16).reshape(packed) * gate).reshape(nk, LANES)

    for par in range(2):
        for part in range(DENSE_PARTS):
            pl.when((s + 2 * part) % 2 == par)(functools.partial(stage, par, part))

    @pl.when((phase == 1 % nch) & (s > 1))
    def _():
        y = alpha * xb_ref[...] + acc_scr[...].T
        o_ref[...] = _layer_norm(y, lg_ref[...], lb_ref[...])


def _dense(x1, am, cnt, e1, rk, u, vt, lg, lb, tb, ec, alpha):
    n, d = x1.shape
    n_exp = u.shape[0]
    nk = PEER_N_KEYS
    nch = n_exp // ec
    assert nch % 2 == 0 and nch >= 4
    total = (n // tb) * nch

    def m1(s):
        return jnp.minimum(s, total - 1)

    def gt(s):
        return jnp.clip(s - 1, 0, total - 1)

    def m2(s):
        return jnp.clip(s - 2, 0, total - 1)

    rspec = pl.BlockSpec((PEER_HEADS, nk // 2, tb), lambda s: (0, 0, gt(s) // nch))
    cspec = pl.BlockSpec((ec // nk, PEER_HEADS, tb), lambda s: (gt(s) % nch, 0, gt(s) // nch))
    return pl.pallas_call(
        functools.partial(_dense_kernel, tb=tb, ec=ec, nch=nch, alpha=alpha),
        grid=(total + 2,),
        in_specs=[pl.BlockSpec((tb, d), lambda s: (m1(s) // nch, 0)),
                  pl.BlockSpec((tb, d), lambda s: (m2(s) // nch, 0)),
                  cspec, cspec, rspec, rspec,
                  pl.BlockSpec((ec, d), lambda s: (m1(s) % nch, 0)),
                  pl.BlockSpec((d, ec), lambda s: (0, m2(s) % nch)),
                  pl.BlockSpec(lg.shape, lambda s: (0, 0)),
                  pl.BlockSpec(lb.shape, lambda s: (0, 0))],
        out_specs=pl.BlockSpec((tb, d), lambda s: (m2(s) // nch, 0)),
        out_shape=jax.ShapeDtypeStruct((n, d), F32),
        scratch_shapes=[pltpu.VMEM((d, tb), BF16),
                        pltpu.VMEM((ec, tb), F32), pltpu.VMEM((ec, tb), F32),
                        pltpu.VMEM((ec, tb), BF16), pltpu.VMEM((ec, tb), BF16),
                        pltpu.VMEM((d, tb), F32)],
        compiler_params=pltpu.CompilerParams(dimension_semantics=("arbitrary",),
                                             vmem_limit_bytes=VMEM_LIMIT),
        name="peer_dense",
    )(x1, x1, am, cnt, e1, rk, u, vt, lg, lb)


def kernel(x, w_in, w_o, attn_sink, na_rpb, t5_table, gnorm_a, gnorm_b, ln1_g, ln1_b, ln2_g, ln2_b,
           peer_wq, peer_keys, peer_u, peer_v):
    b, seq, d = x.shape
    depth = w_in.shape[0]
    n = b * seq
    alpha = (2 * depth) ** 0.25
    tq = min(512, seq)
    tm = min(512, n)
    tb_route = min(256, n)
    tb_dense = min(512, n)
    ec = 2048
    assert seq % tq == 0 and seq // GRID_W >= NA_KEY_ROWS and seq // BLOCK >= 3

    wb = _window_bias_table(t5_table)
    cb_all = _na_bias_table(na_rpb)
    row = lambda v: v.reshape(1, -1)

    for l in range(depth):
        hq, hkv = _proj(x.reshape(n, d), _proj_weight(w_in[l]), tm)
        x1 = _attn(attn_sink[l], hq.reshape(b, seq, Q_WIDTH), hkv.reshape(b, seq, KV_WIDTH), x,
                   wb, cb_all[l], w_o[l].astype(BF16), row(gnorm_a[l]), row(gnorm_b[l]),
                   row(ln1_g[l]), row(ln1_b[l]), tq, alpha)
        x1 = x1.reshape(n, d)
        wq_t, kbd = _route_weights(peer_wq[l], peer_keys[l])
        am, cnt, e1, rk = _route(x1, wq_t, kbd, tb_route)
        x = _dense(x1, am, cnt, e1, rk, peer_u[l].astype(BF16), peer_v[l].T.astype(BF16),
                   row(ln2_g[l]), row(ln2_b[l]), tb_dense, ec, alpha).reshape(b, seq, d)
    return x
```

```python
import functools
import math

import numpy as np
import jax
import jax.numpy as jnp
from jax import lax
from jax.experimental import pallas as pl
from jax.experimental.pallas import tpu as pltpu

F32 = jnp.float32
BF16 = jnp.bfloat16

HEAD_DIM = 64
A_HEADS = 8
A_KV_HEADS = 2
B_HEADS = 8
A_WIDTH = A_HEADS * HEAD_DIM
A_KV_WIDTH = A_KV_HEADS * HEAD_DIM
B_WIDTH = B_HEADS * HEAD_DIM
WINDOW = 128
BLOCK = 128
GRID_W = 64
NA_ROWS = 8
NA_COLS = 16
T5_BUCKETS = 32
T5_MAX_DIST = 128
PEER_HEADS = 8
PEER_N_KEYS = 128
PEER_TOPK = 16
PEER_QDIM = 256
LN_EPS = 1e-5
NEG = -1e30

LANES = 128
BF16_ROWS = 16
DENSE_PARTS = 1
GATE_KEYS = 2
NA_GROUP_ROWS = 4
NA_KEY_ROWS = 12
Q_WIDTH = A_WIDTH + B_WIDTH
KV_WIDTH = 2 * A_KV_WIDTH + 2 * A_KV_WIDTH + B_WIDTH + 2 * B_WIDTH
KV_KA, KV_VA, KV_KB, KV_VB = 0, 2 * A_KV_WIDTH, 4 * A_KV_WIDTH, 4 * A_KV_WIDTH + B_WIDTH

VMEM_LIMIT = 56 * 1024 * 1024


def _dot(a, b):
    return jnp.dot(a, b, preferred_element_type=F32)


def _dot_nt(a, b):
    return lax.dot_general(a, b, (((1,), (1,)), ((), ())), preferred_element_type=F32)


def _layer_norm(y, g, b):
    mu = jnp.mean(y, axis=-1, keepdims=True)
    d = y - mu
    var = jnp.mean(d * d, axis=-1, keepdims=True)
    return d * lax.rsqrt(var + LN_EPS) * g + b


def _rms_norm(y, g):
    return y * lax.rsqrt(jnp.mean(y * y, axis=-1, keepdims=True) + LN_EPS) * g


def _lookup_kernel(tab_ref, idx_ref, mask_ref, o_ref):
    n_entries = tab_ref.shape[1]
    width = idx_ref.shape[1]
    onehot = (lax.broadcasted_iota(jnp.int32, (n_entries, width), 0) == idx_ref[...]).astype(F32)
    val = jnp.dot(tab_ref[...], onehot, preferred_element_type=F32, precision=lax.Precision.HIGHEST)
    o_ref[...] = jnp.where(mask_ref[...] > 0, val, NEG)


def _lookup(tab, idx, mask, chunk):
    rows, n_entries = tab.shape
    width = idx.shape[1]
    return pl.pallas_call(
        _lookup_kernel,
        grid=(width // chunk,),
        in_specs=[pl.BlockSpec((rows, n_entries), lambda i: (0, 0)),
                  pl.BlockSpec((1, chunk), lambda i: (0, i)),
                  pl.BlockSpec((1, chunk), lambda i: (0, i))],
        out_specs=pl.BlockSpec((rows, chunk), lambda i: (0, i)),
        out_shape=jax.ShapeDtypeStruct((rows, width), F32),
        name="bias_lookup",
    )(tab, idx, mask)


def _t5_bucket(rel):
    nb = T5_BUCKETS // 2
    max_exact = nb // 2
    ret = (rel > 0).astype(jnp.int32) * nb
    n = jnp.abs(rel).astype(jnp.int32)
    nf = jnp.maximum(n, 1).astype(jnp.float32)
    large = max_exact + (jnp.log(nf / max_exact) / math.log(T5_MAX_DIST / max_exact)
                         * (nb - max_exact)).astype(jnp.int32)
    large = jnp.minimum(large, nb - 1)
    return ret + jnp.where(n < max_exact, n, large)


def _window_bias_table(t5_table):
    q = jnp.arange(BLOCK)[:, None]
    s = jnp.arange(3 * BLOCK)[None, :]
    rel = jnp.stack([s - q, s - BLOCK - q, s - 2 * BLOCK - q])
    idx = _t5_bucket(rel).reshape(1, -1)
    mask = (jnp.abs(rel) <= WINDOW).astype(F32).reshape(1, -1)
    tab = _lookup(t5_table.T.astype(F32), idx, mask, 3 * BLOCK * 32)
    tab = tab.reshape(A_KV_HEADS, 2, 2, 3, BLOCK, 3 * BLOCK)
    tab = tab.transpose(3, 0, 2, 1, 4, 5)
    return tab.reshape(3, A_KV_HEADS, 2, 2 * BLOCK, 3 * BLOCK)


def _na_bias_table(na_rpb):
    depth = na_rpb.shape[0]
    n_dr, n_dc = 2 * NA_ROWS - 1, 2 * NA_COLS - 1
    c = jnp.arange(GRID_W)
    col_start = jnp.clip(c - NA_COLS // 2, 0, GRID_W - NA_COLS)
    col_mask = (c[None, :] >= col_start[:, None]) & (c[None, :] < col_start[:, None] + NA_COLS)
    dc = jnp.clip(c[None, :] - c[:, None], -(NA_COLS - 1), NA_COLS - 1) + (NA_COLS - 1)
    tab = jnp.pad(na_rpb.reshape(depth * B_HEADS * n_dr, n_dc).astype(F32), ((0, 0), (0, 1)))
    out = _lookup(tab, dc.reshape(1, -1).astype(jnp.int32), col_mask.astype(F32).reshape(1, -1),
                  GRID_W * GRID_W)
    out = out.reshape(depth, B_HEADS, n_dr, GRID_W, GRID_W)
    out = jnp.pad(out, ((0, 0), (0, 0), (1, 2), (0, 0), (0, 0)))
    return jnp.concatenate([out[:, :, :-1], out[:, :, 1:]], axis=-1)


def _proj_kernel(x_ref, w_ref, hq_ref, hkv_ref):
    h = _dot(x_ref[...].astype(BF16), w_ref[...])
    hq_ref[...] = (h[:, :Q_WIDTH] * (HEAD_DIM ** -0.5)).astype(BF16)
    hkv_ref[...] = h[:, Q_WIDTH:].astype(BF16)


def _proj(x2d, w_cat, tm):
    n, d = x2d.shape
    return pl.pallas_call(
        _proj_kernel,
        grid=(n // tm,),
        in_specs=[pl.BlockSpec((tm, d), lambda i: (i, 0)),
                  pl.BlockSpec(w_cat.shape, lambda i: (0, 0))],
        out_specs=[pl.BlockSpec((tm, Q_WIDTH), lambda i: (i, 0)),
                   pl.BlockSpec((tm, KV_WIDTH), lambda i: (i, 0))],
        out_shape=[jax.ShapeDtypeStruct((n, Q_WIDTH), BF16),
                   jax.ShapeDtypeStruct((n, KV_WIDTH), BF16)],
        compiler_params=pltpu.CompilerParams(dimension_semantics=("parallel",),
                                             vmem_limit_bytes=VMEM_LIMIT),
        name="proj",
    )(x2d, w_cat)


def _proj_weight(w_in_l):
    o1 = A_WIDTH
    o2 = o1 + A_KV_WIDTH
    o3 = o2 + A_KV_WIDTH
    o4 = o3 + B_WIDTH
    o5 = o4 + B_WIDTH

    def dup(w, heads):
        w = w.reshape(w.shape[0], heads, 1, HEAD_DIM)
        return jnp.broadcast_to(w, (w.shape[0], heads, 2, HEAD_DIM)).reshape(w.shape[0], heads * 2 * HEAD_DIM)

    cols = [w_in_l[:, :o1], w_in_l[:, o3:o4],
            dup(w_in_l[:, o1:o2], A_KV_HEADS), dup(w_in_l[:, o2:o3], A_KV_HEADS),
            w_in_l[:, o4:o5], dup(w_in_l[:, o5:], B_HEADS)]
    return jnp.concatenate(cols, axis=1).astype(BF16)


def _attn_kernel(sink_ref, hq_ref, hkv_ref, x_ref, wb_ref, cb_ref, wo_ref, ga_ref, gb_ref,
                 lg_ref, lb_ref, o_ref, ya_scr, yb_scr, *, seq, tq, alpha):
    t = pl.program_id(1)
    n_blocks = seq // BLOCK
    rows = seq // GRID_W
    low_half = lax.broadcasted_iota(jnp.int32, (1, LANES), 1) < HEAD_DIM

    def split_halves(kd):
        zero = jnp.zeros_like(kd)
        return jnp.where(low_half, kd, zero), jnp.where(low_half, zero, kd)

    top_rows = lax.broadcasted_iota(jnp.int32, (2 * BLOCK, 1), 0) < BLOCK
    for j in range(tq // BLOCK):
        nb = t * (tq // BLOCK) + j
        kstart = pl.multiple_of(jnp.clip(nb - 1, 0, n_blocks - 3) * BLOCK, BLOCK)
        var = jnp.where(nb == 0, 0, jnp.where(nb == n_blocks - 1, 2, 1))
        qrows = slice(j * BLOCK, (j + 1) * BLOCK)
        for kv in range(A_KV_HEADS):
            c0 = kv * 2 * LANES
            q2 = jnp.concatenate([hq_ref[0, qrows, c0:c0 + LANES],
                                  hq_ref[0, qrows, c0 + LANES:c0 + 2 * LANES]], axis=0)
            kd = hkv_ref[0, pl.ds(kstart, 3 * BLOCK), KV_KA + kv * LANES:KV_KA + (kv + 1) * LANES]
            vd = hkv_ref[0, pl.ds(kstart, 3 * BLOCK), KV_VA + kv * LANES:KV_VA + (kv + 1) * LANES]
            outs = []
            for par, kk in enumerate(split_halves(kd)):
                s = _dot_nt(q2, kk) + wb_ref[var, kv, par]
                sk = jnp.where(top_rows, sink_ref[4 * kv + par], sink_ref[4 * kv + par + 2])
                m = jnp.maximum(jnp.max(s, axis=-1, keepdims=True), sk)
                e = jnp.exp(s - m)
                den = jnp.sum(e, axis=-1, keepdims=True) + jnp.exp(sk - m)
                p = e * (1.0 / den)
                outs.append(_dot(p.astype(BF16), vd))
            slab = jnp.where(low_half, outs[0], outs[1])
            ya_scr[qrows, c0:c0 + LANES] = slab[:BLOCK]
            ya_scr[qrows, c0 + LANES:c0 + 2 * LANES] = slab[BLOCK:]

    gq = NA_GROUP_ROWS * GRID_W
    gk = NA_KEY_ROWS * GRID_W
    for grp in range(tq // gq):
        r0 = (t * (tq // gq) + grp) * NA_GROUP_ROWS
        ws = jnp.clip(r0 - NA_ROWS // 2, 0, rows - NA_KEY_ROWS)
        sd = ws - r0
        kbase = pl.multiple_of(ws * GRID_W, GRID_W)
        shift = GRID_W.bit_length() - 1
        r = r0 + (lax.broadcasted_iota(jnp.int32, (gq, 1), 0) >> shift)
        ka = ws + (lax.broadcasted_iota(jnp.int32, (1, gk), 1) >> shift)
        rs = jnp.clip(r - NA_ROWS // 2, 0, rows - NA_ROWS)
        row_mask = jnp.where((ka >= rs) & (ka < rs + NA_ROWS), 0.0, NEG)
        grows = slice(grp * gq, (grp + 1) * gq)
        for a in range(B_HEADS // 2):
            qs = hq_ref[0, grows, A_WIDTH + a * LANES:A_WIDTH + (a + 1) * LANES]
            kd = hkv_ref[0, pl.ds(kbase, gk), KV_KB + a * LANES:KV_KB + (a + 1) * LANES]
            outs = []
            for par, kk in enumerate(split_halves(kd)):
                h = 2 * a + par
                bias_rows = []
                for rq in range(NA_GROUP_ROWS):
                    blocks = []
                    for kp in range(NA_KEY_ROWS // 2):
                        e = jnp.clip(sd + (2 * kp - rq + 8), 0, 16)
                        blocks.append(cb_ref[h, e])
                    bias_rows.append(jnp.concatenate(blocks, axis=1))
                bias = jnp.concatenate(bias_rows, axis=0)
                s = _dot_nt(qs, kk) + bias + row_mask
                m = jnp.max(s, axis=-1, keepdims=True)
                e = jnp.exp(s - m)
                p = e * (1.0 / jnp.sum(e, axis=-1, keepdims=True))
                vd = hkv_ref[0, pl.ds(kbase, gk), KV_VB + h * LANES:KV_VB + (h + 1) * LANES]
                outs.append(_dot(p.astype(BF16), vd))
            yb_scr[grows, a * LANES:(a + 1) * LANES] = jnp.where(low_half, outs[0], outs[1])

    na = _rms_norm(ya_scr[...], ga_ref[...])
    nb_ = _rms_norm(yb_scr[...], gb_ref[...])
    mix = jnp.concatenate([na, nb_], axis=-1).astype(BF16)
    y = alpha * x_ref[0] + _dot(mix, wo_ref[...])
    o_ref[0] = _layer_norm(y, lg_ref[...], lb_ref[...])


def _attn(sink, hq, hkv, x3d, wb, cb, wo, ga, gb, lg, lb, tq, alpha):
    b, seq, d = x3d.shape
    const = lambda *shape: pl.BlockSpec(shape, lambda i, j, s: (0,) * len(shape),
                                        pipeline_mode=pl.Buffered(1))
    grid_spec = pltpu.PrefetchScalarGridSpec(
        num_scalar_prefetch=1,
        grid=(b, seq // tq),
        in_specs=[pl.BlockSpec((1, tq, Q_WIDTH), lambda i, j, s: (i, j, 0)),
                  pl.BlockSpec((1, seq, KV_WIDTH), lambda i, j, s: (i, 0, 0),
                               pipeline_mode=pl.Buffered(1)),
                  pl.BlockSpec((1, tq, d), lambda i, j, s: (i, j, 0)),
                  const(*wb.shape), const(*cb.shape), const(*wo.shape),
                  const(*ga.shape), const(*gb.shape), const(*lg.shape), const(*lb.shape)],
        out_specs=pl.BlockSpec((1, tq, d), lambda i, j, s: (i, j, 0)),
        scratch_shapes=[pltpu.VMEM((tq, A_WIDTH), F32), pltpu.VMEM((tq, B_WIDTH), F32)],
    )
    return pl.pallas_call(
        functools.partial(_attn_kernel, seq=seq, tq=tq, alpha=alpha),
        grid_spec=grid_spec,
        out_shape=jax.ShapeDtypeStruct((b, seq, d), F32),
        compiler_params=pltpu.CompilerParams(dimension_semantics=("parallel", "arbitrary"),
                                             vmem_limit_bytes=VMEM_LIMIT),
        name="attn",
    )(sink, hq, hkv, x3d, wb, cb, wo, ga, gb, lg, lb)


def _tree(op, xs):
    xs = list(xs)
    while len(xs) > 1:
        xs = [op(xs[i], xs[i + 1]) for i in range(0, len(xs) - 1, 2)] + ([xs[-1]] if len(xs) % 2 else [])
    return xs[0]


def _route_kernel(x_ref, wq_ref, kbd_ref, am_ref, cnt_ref, e1_ref, rk_ref,
                  q_scr, s_scr, w_scr, rank_scr, vals_scr, e_scr, *, tb):
    k = PEER_TOPK
    nk = PEER_N_KEYS
    nh = PEER_HEADS
    half = nh * (PEER_QDIM // 2)
    xt = x_ref[...].T.astype(BF16)
    q_scr[...] = _dot(wq_ref[...], xt).astype(BF16)
    for p in range(2):
        s_scr[p] = _dot(kbd_ref[p], q_scr[p * half:(p + 1) * half, :])

    def slab(kk):
        return slice(kk * nh, (kk + 1) * nh)

    for lb in range(tb // LANES):
        lanes = slice(lb * LANES, (lb + 1) * LANES)

        for p in range(2):
            w_scr[...] = s_scr[p, :, lanes]
            rank_scr[p] = jnp.full((nk * nh, LANES), 99.0, F32)

            def round_body(a, _, p=p):
                vals = [w_scr[slab(kk), :] for kk in range(nk)]
                m = _tree(jnp.maximum, vals)
                idx = _tree(jnp.minimum, [jnp.where(v == m, float(kk), float(nk))
                                          for kk, v in enumerate(vals)])
                vals_scr[p, pl.ds(pl.multiple_of(a * nh, nh), nh), :] = m
                af = lax.convert_element_type(a, F32)
                for kk in range(nk):
                    hit = idx == float(kk)
                    w_scr[slab(kk), :] = jnp.where(hit, -jnp.inf, vals[kk])
                    rank_scr[p, slab(kk), :] = jnp.where(hit, af, rank_scr[p, slab(kk), :])
                return 0

            lax.fori_loop(0, k, round_body, 0)

        v0 = vals_scr[0].reshape(k, nh, LANES)
        v1 = vals_scr[1].reshape(k, nh, LANES)
        iota_a = lax.broadcasted_iota(jnp.int32, (k, nh, LANES), 0).astype(F32)

        def merge_body(_, carry):
            cnt, front = carry
            m = jnp.max(front, axis=0, keepdims=True)
            asel = jnp.min(jnp.where(front == m, iota_a, 99.0), axis=0, keepdims=True)
            hit = iota_a == asel
            cnt = cnt + hit.astype(F32)
            csel = jnp.sum(jnp.where(hit, cnt, 0.0), axis=0, keepdims=True)
            nxt = jnp.sum(jnp.where(iota_a == csel, v1, 0.0), axis=0, keepdims=True)
            nxt = jnp.where(csel >= k, -jnp.inf, nxt)
            return cnt, jnp.where(hit, v0 + nxt, front)

        cnt, _ = lax.fori_loop(0, k, merge_body, (jnp.zeros((k, nh, LANES), F32), v0 + v1[0:1]))

        e0 = jnp.exp(v0 - v0[0:1])
        e1 = jnp.exp(v1 - v1[0:1])
        pref = jnp.zeros((k, nh, LANES), F32)
        for b in range(k):
            pref = pref + jnp.where(cnt > b, e1[b:b + 1], 0.0)
        inv_z = 1.0 / jnp.sum(e0 * pref, axis=0)

        for kk in range(nk):
            rank0 = rank_scr[0, slab(kk), :]
            gate0 = jnp.exp(s_scr[0, slab(kk), lanes] - v0[0]) * inv_z
            am_ref[kk, :, lanes] = jnp.where(rank0 < k, gate0, 0.0)
            cnt_ref[kk, :, lanes] = _tree(jnp.add, [jnp.where(rank0 == a, cnt[a], 0.0) for a in range(k)])
            e_scr[slab(kk), :] = jnp.exp(s_scr[1, slab(kk), lanes] - v1[0])
        for h in range(nh):
            e1_ref[h, :, lanes] = pltpu.bitcast(
                e_scr[pl.ds(h, nk, stride=nh), :].astype(BF16), jnp.uint32)
            rk_ref[h, :, lanes] = pltpu.bitcast(
                rank_scr[1, pl.ds(h, nk, stride=nh), :].astype(BF16), jnp.uint32)


def _route(x1, wq_t, kbd, tb):
    n, d = x1.shape
    nk = PEER_N_KEYS
    nh = PEER_HEADS
    by_key = jax.ShapeDtypeStruct((nk, nh, n), F32)
    by_head = jax.ShapeDtypeStruct((nh, nk // 2, n), jnp.uint32)
    kspec = pl.BlockSpec((nk, nh, tb), lambda i: (0, 0, i))
    hspec = pl.BlockSpec((nh, nk // 2, tb), lambda i: (0, 0, i))
    return pl.pallas_call(
        functools.partial(_route_kernel, tb=tb),
        grid=(n // tb,),
        in_specs=[pl.BlockSpec((tb, d), lambda i: (i, 0)),
                  pl.BlockSpec(wq_t.shape, lambda i: (0, 0)),
                  pl.BlockSpec(kbd.shape, lambda i: (0, 0, 0))],
        out_specs=[kspec, kspec, hspec, hspec],
        out_shape=[by_key, by_key, by_head, by_head],
        scratch_shapes=[pltpu.VMEM((2 * nh * nk, tb), BF16),
                        pltpu.VMEM((2, nk * nh, tb), F32),
                        pltpu.VMEM((nk * nh, LANES), F32),
                        pltpu.VMEM((2, nk * nh, LANES), F32),
                        pltpu.VMEM((2, PEER_TOPK * nh, LANES), F32),
                        pltpu.VMEM((nk * nh, LANES), F32)],
        compiler_params=pltpu.CompilerParams(dimension_semantics=("parallel",),
                                             vmem_limit_bytes=VMEM_LIMIT),
        name="peer_route",
    )(x1, wq_t, kbd)


def _route_weights(peer_wq_l, peer_keys_l):
    nh, nk, c = PEER_HEADS, PEER_N_KEYS, PEER_QDIM // 2
    d = peer_wq_l.shape[0]
    wq_t = peer_wq_l.T.reshape(nh, 2, c, d).transpose(1, 0, 2, 3).reshape(2 * nh * c, d)
    kbd = jnp.einsum('hpkc,hg->pkhgc', peer_keys_l, jnp.eye(nh, dtype=peer_keys_l.dtype))
    return wq_t.astype(BF16), kbd.reshape(2, nk * nh, nh * c).astype(BF16)


def _dense_kernel(xa_ref, xb_ref, am_ref, cnt_ref, e1_ref, rk_ref, u_ref, vt_ref, lg_ref, lb_ref,
                  o_ref, xt_scr, h0_scr, h1_scr, a0_scr, a1_scr, acc_scr, *, tb, ec, nch, alpha):
    s = pl.program_id(0)
    nk = PEER_N_KEYS
    phase = s % nch
    h_bufs = (h0_scr, h1_scr)
    a_bufs = (a0_scr, a1_scr)

    @pl.when(s == 0)
    def _():
        for buf in h_bufs + a_bufs + (acc_scr,):
            buf[...] = jnp.zeros_like(buf)

    @pl.when(phase == 0)
    def _():
        xt_scr[...] = xa_ref[...].T.astype(BF16)

    @pl.when(phase == 2 % nch)
    def _():
        acc_scr[...] = jnp.zeros_like(acc_scr)

    def stage(cur, part):
        h_cur, h_prv = h_bufs[cur], h_bufs[1 - cur]
        a_cur, a_prv = a_bufs[cur], a_bufs[1 - cur]
        tw = tb // DENSE_PARTS
        toks = slice(part * tw, (part + 1) * tw)
        acc_scr[:, toks] += _dot(vt_ref[...], a_cur[:, toks])
        h_cur[:, toks] = _dot(u_ref[...], xt_scr[:, toks])
        packed = (nk // BF16_ROWS, BF16_ROWS, LANES)
        zero = jnp.zeros((), BF16)
        for i0 in range(0, ec // nk, GATE_KEYS):
            keys = range(i0, i0 + GATE_KEYS)
            for lb in range(part * tw // LANES, (part + 1) * tw // LANES):
                lanes = slice(lb * LANES, (lb + 1) * LANES)
                gates = [jnp.zeros(packed, BF16) for _ in keys]
                for h in range(PEER_HEADS):
                    rk = pltpu.bitcast(rk_ref[h, :, lanes], BF16).reshape(packed)
                    e1 = pltpu.bitcast(e1_ref[h, :, lanes], BF16).reshape(packed)
                    for g, ii in enumerate(keys):
                        am = jnp.broadcast_to(am_ref[ii, h:h + 1, lanes], packed[1:]).astype(BF16)
                        cnt = jnp.broadcast_to(cnt_ref[ii, h:h + 1, lanes], packed[1:]).astype(BF16)
                        gates[g] = gates[g] + jnp.where(cnt[None] - rk > 0, am[None] * e1, zero)
                for g, ii in enumerate(keys):
                    rows = slice(ii * nk, (ii + 1) * nk)
                    hv = h_prv[rows, lanes]
                    half = 0.5 * hv
                    gelu = half + half * lax.erf(hv * (0.5 ** 0.5))
                    a_prv[rows, lanes] = (gelu.astype(BF16).reshape(packed) * gates[g]).reshape(nk, LANES)

    for par in range(2):
        for part in range(DENSE_PARTS):
            pl.when((s + 2 * part) % 2 == par)(functools.partial(stage, par, part))

    @pl.when((phase == 1 % nch) & (s > 1))
    def _():
        y = alpha * xb_ref[...] + acc_scr[...].T
        o_ref[...] = _layer_norm(y, lg_ref[...], lb_ref[...])


def _dense(x1, am, cnt, e1, rk, u, vt, lg, lb, tb, ec, alpha):
    n, d = x1.shape
    n_exp = u.shape[0]
    nk = PEER_N_KEYS
    nch = n_exp // ec
    assert nch % 2 == 0 and nch >= 4
    total = (n // tb) * nch

    def m1(s):
        return jnp.minimum(s, total - 1)

    def gt(s):
        return jnp.clip(s - 1, 0, total - 1)

    def m2(s):
        return jnp.clip(s - 2, 0, total - 1)

    rspec = pl.BlockSpec((PEER_HEADS, nk // 2, tb), lambda s: (0, 0, gt(s) // nch))
    cspec = pl.BlockSpec((ec // nk, PEER_HEADS, tb), lambda s: (gt(s) % nch, 0, gt(s) // nch))
    return pl.pallas_call(
        functools.partial(_dense_kernel, tb=tb, ec=ec, nch=nch, alpha=alpha),
        grid=(total + 2,),
        in_specs=[pl.BlockSpec((tb, d), lambda s: (m1(s) // nch, 0)),
                  pl.BlockSpec((tb, d), lambda s: (m2(s) // nch, 0)),
                  cspec, cspec, rspec, rspec,
                  pl.BlockSpec((ec, d), lambda s: (m1(s) % nch, 0)),
                  pl.BlockSpec((d, ec), lambda s: (0, m2(s) % nch)),
                  pl.BlockSpec(lg.shape, lambda s: (0, 0)),
                  pl.BlockSpec(lb.shape, lambda s: (0, 0))],
        out_specs=pl.BlockSpec((tb, d), lambda s: (m2(s) // nch, 0)),
        out_shape=jax.ShapeDtypeStruct((n, d), F32),
        scratch_shapes=[pltpu.VMEM((d, tb), BF16),
                        pltpu.VMEM((ec, tb), F32), pltpu.VMEM((ec, tb), F32),
                        pltpu.VMEM((ec, tb), BF16), pltpu.VMEM((ec, tb), BF16),
                        pltpu.VMEM((d, tb), F32)],
        compiler_params=pltpu.CompilerParams(dimension_semantics=("arbitrary",),
                                             vmem_limit_bytes=VMEM_LIMIT),
        name="peer_dense",
    )(x1, x1, am, cnt, e1, rk, u, vt, lg, lb)


def kernel(x, w_in, w_o, attn_sink, na_rpb, t5_table, gnorm_a, gnorm_b, ln1_g, ln1_b, ln2_g, ln2_b,
           peer_wq, peer_keys, peer_u, peer_v):
    b, seq, d = x.shape
    depth = w_in.shape[0]
    n = b * seq
    alpha = (2 * depth) ** 0.25
    tq = min(512, seq)
    tm = min(512, n)
    tb_route = min(256, n)
    tb_dense = min(512, n)
    ec = 2048
    assert seq % tq == 0 and seq // GRID_W >= NA_KEY_ROWS and seq // BLOCK >= 3

    wb = _window_bias_table(t5_table)
    cb_all = _na_bias_table(na_rpb)
    row = lambda v: v.reshape(1, -1)

    for l in range(depth):
        hq, hkv = _proj(x.reshape(n, d), _proj_weight(w_in[l]), tm)
        x1 = _attn(attn_sink[l], hq.reshape(b, seq, Q_WIDTH), hkv.reshape(b, seq, KV_WIDTH), x,
                   wb, cb_all[l], w_o[l].astype(BF16), row(gnorm_a[l]), row(gnorm_b[l]),
                   row(ln1_g[l]), row(ln1_b[l]), tq, alpha)
        x1 = x1.reshape(n, d)
        wq_t, kbd = _route_weights(peer_wq[l], peer_keys[l])
        am, cnt, e1, rk = _route(x1, wq_t, kbd, tb_route)
        x = _dense(x1, am, cnt, e1, rk, peer_u[l].astype(BF16), peer_v[l].T.astype(BF16),
                   row(ln2_g[l]), row(ln2_b[l]), tb_dense, ec, alpha).reshape(b, seq, d)
    return x
```

```python
import functools
import math

import numpy as np
import jax
import jax.numpy as jnp
from jax import lax
from jax.experimental import pallas as pl
from jax.experimental.pallas import tpu as pltpu

F32 = jnp.float32
BF16 = jnp.bfloat16

HEAD_DIM = 64
A_HEADS = 8
A_KV_HEADS = 2
B_HEADS = 8
A_WIDTH = A_HEADS * HEAD_DIM
A_KV_WIDTH = A_KV_HEADS * HEAD_DIM
B_WIDTH = B_HEADS * HEAD_DIM
WINDOW = 128
BLOCK = 128
GRID_W = 64
NA_ROWS = 8
NA_COLS = 16
T5_BUCKETS = 32
T5_MAX_DIST = 128
PEER_HEADS = 8
PEER_N_KEYS = 128
PEER_TOPK = 16
PEER_QDIM = 256
LN_EPS = 1e-5
NEG = -1e30

LANES = 128
BF16_ROWS = 16
DENSE_PARTS = 1
GATE_KEYS = 2
NA_GROUP_ROWS = 4
NA_KEY_ROWS = 12
Q_WIDTH = A_WIDTH + B_WIDTH
KV_WIDTH = 2 * A_KV_WIDTH + 2 * A_KV_WIDTH + B_WIDTH + 2 * B_WIDTH
KV_KA, KV_VA, KV_KB, KV_VB = 0, 2 * A_KV_WIDTH, 4 * A_KV_WIDTH, 4 * A_KV_WIDTH + B_WIDTH

VMEM_LIMIT = 56 * 1024 * 1024


def _dot(a, b):
    return jnp.dot(a, b, preferred_element_type=F32)


def _dot_nt(a, b):
    return lax.dot_general(a, b, (((1,), (1,)), ((), ())), preferred_element_type=F32)


def _layer_norm(y, g, b):
    mu = jnp.mean(y, axis=-1, keepdims=True)
    d = y - mu
    var = jnp.mean(d * d, axis=-1, keepdims=True)
    return d * lax.rsqrt(var + LN_EPS) * g + b


def _rms_norm(y, g):
    return y * lax.rsqrt(jnp.mean(y * y, axis=-1, keepdims=True) + LN_EPS) * g


def _lookup_kernel(tab_ref, idx_ref, mask_ref, o_ref):
    n_entries = tab_ref.shape[1]
    width = idx_ref.shape[1]
    onehot = (lax.broadcasted_iota(jnp.int32, (n_entries, width), 0) == idx_ref[...]).astype(F32)
    val = jnp.dot(tab_ref[...], onehot, preferred_element_type=F32, precision=lax.Precision.HIGHEST)
    o_ref[...] = jnp.where(mask_ref[...] > 0, val, NEG)


def _lookup(tab, idx, mask, chunk):
    rows, n_entries = tab.shape
    width = idx.shape[1]
    return pl.pallas_call(
        _lookup_kernel,
        grid=(width // chunk,),
        in_specs=[pl.BlockSpec((rows, n_entries), lambda i: (0, 0)),
                  pl.BlockSpec((1, chunk), lambda i: (0, i)),
                  pl.BlockSpec((1, chunk), lambda i: (0, i))],
        out_specs=pl.BlockSpec((rows, chunk), lambda i: (0, i)),
        out_shape=jax.ShapeDtypeStruct((rows, width), F32),
        name="bias_lookup",
    )(tab, idx, mask)


def _t5_bucket(rel):
    nb = T5_BUCKETS // 2
    max_exact = nb // 2
    ret = (rel > 0).astype(jnp.int32) * nb
    n = jnp.abs(rel).astype(jnp.int32)
    nf = jnp.maximum(n, 1).astype(jnp.float32)
    large = max_exact + (jnp.log(nf / max_exact) / math.log(T5_MAX_DIST / max_exact)
                         * (nb - max_exact)).astype(jnp.int32)
    large = jnp.minimum(large, nb - 1)
    return ret + jnp.where(n < max_exact, n, large)


def _window_bias_table(t5_table):
    q = jnp.arange(BLOCK)[:, None]
    s = jnp.arange(3 * BLOCK)[None, :]
    rel = jnp.stack([s - q, s - BLOCK - q, s - 2 * BLOCK - q])
    idx = _t5_bucket(rel).reshape(1, -1)
    mask = (jnp.abs(rel) <= WINDOW).astype(F32).reshape(1, -1)
    tab = _lookup(t5_table.T.astype(F32), idx, mask, 3 * BLOCK * 32)
    tab = tab.reshape(A_KV_HEADS, 2, 2, 3, BLOCK, 3 * BLOCK)
    tab = tab.transpose(3, 0, 2, 1, 4, 5)
    return tab.reshape(3, A_KV_HEADS, 2, 2 * BLOCK, 3 * BLOCK)


def _na_bias_table(na_rpb):
    depth = na_rpb.shape[0]
    n_dr, n_dc = 2 * NA_ROWS - 1, 2 * NA_COLS - 1
    c = jnp.arange(GRID_W)
    col_start = jnp.clip(c - NA_COLS // 2, 0, GRID_W - NA_COLS)
    col_mask = (c[None, :] >= col_start[:, None]) & (c[None, :] < col_start[:, None] + NA_COLS)
    dc = jnp.clip(c[None, :] - c[:, None], -(NA_COLS - 1), NA_COLS - 1) + (NA_COLS - 1)
    tab = jnp.pad(na_rpb.reshape(depth * B_HEADS * n_dr, n_dc).astype(F32), ((0, 0), (0, 1)))
    out = _lookup(tab, dc.reshape(1, -1).astype(jnp.int32), col_mask.astype(F32).reshape(1, -1),
                  GRID_W * GRID_W)
    out = out.reshape(depth, B_HEADS, n_dr, GRID_W, GRID_W)
    out = jnp.pad(out, ((0, 0), (0, 0), (1, 2), (0, 0), (0, 0)))
    return jnp.concatenate([out[:, :, :-1], out[:, :, 1:]], axis=-1)


def _proj_kernel(x_ref, w_ref, hq_ref, hkv_ref):
    h = _dot(x_ref[...].astype(BF16), w_ref[...])
    hq_ref[...] = (h[:, :Q_WIDTH] * (HEAD_DIM ** -0.5)).astype(BF16)
    hkv_ref[...] = h[:, Q_WIDTH:].astype(BF16)


def _proj(x2d, w_cat, tm):
    n, d = x2d.shape
    return pl.pallas_call(
        _proj_kernel,
        grid=(n // tm,),
        in_specs=[pl.BlockSpec((tm, d), lambda i: (i, 0)),
                  pl.BlockSpec(w_cat.shape, lambda i: (0, 0))],
        out_specs=[pl.BlockSpec((tm, Q_WIDTH), lambda i: (i, 0)),
                   pl.BlockSpec((tm, KV_WIDTH), lambda i: (i, 0))],
        out_shape=[jax.ShapeDtypeStruct((n, Q_WIDTH), BF16),
                   jax.ShapeDtypeStruct((n, KV_WIDTH), BF16)],
        compiler_params=pltpu.CompilerParams(dimension_semantics=("parallel",),
                                             vmem_limit_bytes=VMEM_LIMIT),
        name="proj",
    )(x2d, w_cat)


def _proj_weight(w_in_l):
    o1 = A_WIDTH
    o2 = o1 + A_KV_WIDTH
    o3 = o2 + A_KV_WIDTH
    o4 = o3 + B_WIDTH
    o5 = o4 + B_WIDTH

    def dup(w, heads):
        w = w.reshape(w.shape[0], heads, 1, HEAD_DIM)
        return jnp.broadcast_to(w, (w.shape[0], heads, 2, HEAD_DIM)).reshape(w.shape[0], heads * 2 * HEAD_DIM)

    cols = [w_in_l[:, :o1], w_in_l[:, o3:o4],
            dup(w_in_l[:, o1:o2], A_KV_HEADS), dup(w_in_l[:, o2:o3], A_KV_HEADS),
            w_in_l[:, o4:o5], dup(w_in_l[:, o5:], B_HEADS)]
    return jnp.concatenate(cols, axis=1).astype(BF16)


def _attn_kernel(sink_ref, hq_ref, hkv_ref, x_ref, wb_ref, cb_ref, wo_ref, ga_ref, gb_ref,
                 lg_ref, lb_ref, o_ref, ya_scr, yb_scr, *, seq, tq, alpha):
    t = pl.program_id(1)
    n_blocks = seq // BLOCK
    rows = seq // GRID_W
    low_half = lax.broadcasted_iota(jnp.int32, (1, LANES), 1) < HEAD_DIM

    def split_halves(kd):
        zero = jnp.zeros_like(kd)
        return jnp.where(low_half, kd, zero), jnp.where(low_half, zero, kd)

    top_rows = lax.broadcasted_iota(jnp.int32, (2 * BLOCK, 1), 0) < BLOCK
    for j in range(tq // BLOCK):
        nb = t * (tq // BLOCK) + j
        kstart = pl.multiple_of(jnp.clip(nb - 1, 0, n_blocks - 3) * BLOCK, BLOCK)
        var = jnp.where(nb == 0, 0, jnp.where(nb == n_blocks - 1, 2, 1))
        qrows = slice(j * BLOCK, (j + 1) * BLOCK)
        for kv in range(A_KV_HEADS):
            c0 = kv * 2 * LANES
            q2 = jnp.concatenate([hq_ref[0, qrows, c0:c0 + LANES],
                                  hq_ref[0, qrows, c0 + LANES:c0 + 2 * LANES]], axis=0)
            kd = hkv_ref[0, pl.ds(kstart, 3 * BLOCK), KV_KA + kv * LANES:KV_KA + (kv + 1) * LANES]
            vd = hkv_ref[0, pl.ds(kstart, 3 * BLOCK), KV_VA + kv * LANES:KV_VA + (kv + 1) * LANES]
            outs = []
            for par, kk in enumerate(split_halves(kd)):
                s = _dot_nt(q2, kk) + wb_ref[var, kv, par]
                sk = jnp.where(top_rows, sink_ref[4 * kv + par], sink_ref[4 * kv + par + 2])
                m = jnp.maximum(jnp.max(s, axis=-1, keepdims=True), sk)
                e = jnp.exp(s - m)
                den = jnp.sum(e, axis=-1, keepdims=True) + jnp.exp(sk - m)
                p = e * (1.0 / den)
                outs.append(_dot(p.astype(BF16), vd))
            slab = jnp.where(low_half, outs[0], outs[1])
            ya_scr[qrows, c0:c0 + LANES] = slab[:BLOCK]
            ya_scr[qrows, c0 + LANES:c0 + 2 * LANES] = slab[BLOCK:]

    gq = NA_GROUP_ROWS * GRID_W
    gk = NA_KEY_ROWS * GRID_W
    for grp in range(tq // gq):
        r0 = (t * (tq // gq) + grp) * NA_GROUP_ROWS
        ws = jnp.clip(r0 - NA_ROWS // 2, 0, rows - NA_KEY_ROWS)
        sd = ws - r0
        kbase = pl.multiple_of(ws * GRID_W, GRID_W)
        shift = GRID_W.bit_length() - 1
        r = r0 + (lax.broadcasted_iota(jnp.int32, (gq, 1), 0) >> shift)
        ka = ws + (lax.broadcasted_iota(jnp.int32, (1, gk), 1) >> shift)
        rs = jnp.clip(r - NA_ROWS // 2, 0, rows - NA_ROWS)
        row_mask = jnp.where((ka >= rs) & (ka < rs + NA_ROWS), 0.0, NEG)
        grows = slice(grp * gq, (grp + 1) * gq)
        for a in range(B_HEADS // 2):
            qs = hq_ref[0, grows, A_WIDTH + a * LANES:A_WIDTH + (a + 1) * LANES]
            kd = hkv_ref[0, pl.ds(kbase, gk), KV_KB + a * LANES:KV_KB + (a + 1) * LANES]
            outs = []
            for par, kk in enumerate(split_halves(kd)):
                h = 2 * a + par
                bias_rows = []
                for rq in range(NA_GROUP_ROWS):
                    blocks = []
                    for kp in range(NA_KEY_ROWS // 2):
                        e = jnp.clip(sd + (2 * kp - rq + 8), 0, 16)
                        blocks.append(cb_ref[h, e])
                    bias_rows.append(jnp.concatenate(blocks, axis=1))
                bias = jnp.concatenate(bias_rows, axis=0)
                s = _dot_nt(qs, kk) + bias + row_mask
                m = jnp.max(s, axis=-1, keepdims=True)
                e = jnp.exp(s - m)
                p = e * (1.0 / jnp.sum(e, axis=-1, keepdims=True))
                vd = hkv_ref[0, pl.ds(kbase, gk), KV_VB + h * LANES:KV_VB + (h + 1) * LANES]
                outs.append(_dot(p.astype(BF16), vd))
            yb_scr[grows, a * LANES:(a + 1) * LANES] = jnp.where(low_half, outs[0], outs[1])

    na = _rms_norm(ya_scr[...], ga_ref[...])
    nb_ = _rms_norm(yb_scr[...], gb_ref[...])
    mix = jnp.concatenate([na, nb_], axis=-1).astype(BF16)
    y = alpha * x_ref[0] + _dot(mix, wo_ref[...])
    o_ref[0] = _layer_norm(y, lg_ref[...], lb_ref[...])


def _attn(sink, hq, hkv, x3d, wb, cb, wo, ga, gb, lg, lb, tq, alpha):
    b, seq, d = x3d.shape
    const = lambda *shape: pl.BlockSpec(shape, lambda i, j, s: (0,) * len(shape),
                                        pipeline_mode=pl.Buffered(1))
    grid_spec = pltpu.PrefetchScalarGridSpec(
        num_scalar_prefetch=1,
        grid=(b, seq // tq),
        in_specs=[pl.BlockSpec((1, tq, Q_WIDTH), lambda i, j, s: (i, j, 0)),
                  pl.BlockSpec((1, seq, KV_WIDTH), lambda i, j, s: (i, 0, 0),
                               pipeline_mode=pl.Buffered(1)),
                  pl.BlockSpec((1, tq, d), lambda i, j, s: (i, j, 0)),
                  const(*wb.shape), const(*cb.shape), const(*wo.shape),
                  const(*ga.shape), const(*gb.shape), const(*lg.shape), const(*lb.shape)],
        out_specs=pl.BlockSpec((1, tq, d), lambda i, j, s: (i, j, 0)),
        scratch_shapes=[pltpu.VMEM((tq, A_WIDTH), F32), pltpu.VMEM((tq, B_WIDTH), F32)],
    )
    return pl.pallas_call(
        functools.partial(_attn_kernel, seq=seq, tq=tq, alpha=alpha),
        grid_spec=grid_spec,
        out_shape=jax.ShapeDtypeStruct((b, seq, d), F32),
        compiler_params=pltpu.CompilerParams(dimension_semantics=("parallel", "arbitrary"),
                                             vmem_limit_bytes=VMEM_LIMIT),
        name="attn",
    )(sink, hq, hkv, x3d, wb, cb, wo, ga, gb, lg, lb)


def _tree(op, xs):
    xs = list(xs)
    while len(xs) > 1:
        xs = [op(xs[i], xs[i + 1]) for i in range(0, len(xs) - 1, 2)] + ([xs[-1]] if len(xs) % 2 else [])
    return xs[0]


def _route_kernel(x_ref, wq_ref, kbd_ref, am_ref, cnt_ref, e1_ref, rk_ref,
                  q_scr, s_scr, w_scr, rank_scr, vals_scr, e_scr, *, tb):
    k = PEER_TOPK
    nk = PEER_N_KEYS
    nh = PEER_HEADS
    half = nh * (PEER_QDIM // 2)
    xt = x_ref[...].T.astype(BF16)
    q_scr[...] = _dot(wq_ref[...], xt).astype(BF16)
    for p in range(2):
        s_scr[p] = _dot(kbd_ref[p], q_scr[p * half:(p + 1) * half, :])

    def slab(kk):
        return slice(kk * nh, (kk + 1) * nh)

    for lb in range(tb // LANES):
        lanes = slice(lb * LANES, (lb + 1) * LANES)

        for p in range(2):
            w_scr[...] = s_scr[p, :, lanes]
            rank_scr[p] = jnp.full((nk * nh, LANES), 99.0, F32)

            def round_body(a, _, p=p):
                vals = [w_scr[slab(kk), :] for kk in range(nk)]
                m = _tree(jnp.maximum, vals)
                idx = _tree(jnp.minimum, [jnp.where(v == m, float(kk), float(nk))
                                          for kk, v in enumerate(vals)])
                vals_scr[p, pl.ds(pl.multiple_of(a * nh, nh), nh), :] = m
                af = lax.convert_element_type(a, F32)
                for kk in range(nk):
                    hit = idx == float(kk)
                    w_scr[slab(kk), :] = jnp.where(hit, -jnp.inf, vals[kk])
                    rank_scr[p, slab(kk), :] = jnp.where(hit, af, rank_scr[p, slab(kk), :])
                return 0

            lax.fori_loop(0, k, round_body, 0)

        v0 = vals_scr[0].reshape(k, nh, LANES)
        v1 = vals_scr[1].reshape(k, nh, LANES)
        iota_a = lax.broadcasted_iota(jnp.int32, (k, nh, LANES), 0).astype(F32)

        def merge_body(_, carry):
            cnt, front = carry
            m = jnp.max(front, axis=0, keepdims=True)
            asel = jnp.min(jnp.where(front == m, iota_a, 99.0), axis=0, keepdims=True)
            hit = iota_a == asel
            cnt = cnt + hit.astype(F32)
            csel = jnp.sum(jnp.where(hit, cnt, 0.0), axis=0, keepdims=True)
            nxt = jnp.sum(jnp.where(iota_a == csel, v1, 0.0), axis=0, keepdims=True)
            nxt = jnp.where(csel >= k, -jnp.inf, nxt)
            return cnt, jnp.where(hit, v0 + nxt, front)

        cnt, _ = lax.fori_loop(0, k, merge_body, (jnp.zeros((k, nh, LANES), F32), v0 + v1[0:1]))

        e0 = jnp.exp(v0 - v0[0:1])
        e1 = jnp.exp(v1 - v1[0:1])
        pref = jnp.zeros((k, nh, LANES), F32)
        for b in range(k):
            pref = pref + jnp.where(cnt > b, e1[b:b + 1], 0.0)
        inv_z = 1.0 / jnp.sum(e0 * pref, axis=0)

        for kk in range(nk):
            rank0 = rank_scr[0, slab(kk), :]
            gate0 = jnp.exp(s_scr[0, slab(kk), lanes] - v0[0]) * inv_z
            am_ref[kk, :, lanes] = jnp.where(rank0 < k, gate0, 0.0)
            cnt_ref[kk, :, lanes] = _tree(jnp.add, [jnp.where(rank0 == a, cnt[a], 0.0) for a in range(k)])
            e_scr[slab(kk), :] = jnp.exp(s_scr[1, slab(kk), lanes] - v1[0])
        for h in range(nh):
            e1_ref[h, :, lanes] = pltpu.bitcast(
                e_scr[pl.ds(h, nk, stride=nh), :].astype(BF16), jnp.uint32)
            rk_ref[h, :, lanes] = pltpu.bitcast(
                rank_scr[1, pl.ds(h, nk, stride=nh), :].astype(BF16), jnp.uint32)


def _route(x1, wq_t, kbd, tb):
    n, d = x1.shape
    nk = PEER_N_KEYS
    nh = PEER_HEADS
    by_key = jax.ShapeDtypeStruct((nk, nh, n), F32)
    by_head = jax.ShapeDtypeStruct((nh, nk // 2, n), jnp.uint32)
    kspec = pl.BlockSpec((nk, nh, tb), lambda i: (0, 0, i))
    hspec = pl.BlockSpec((nh, nk // 2, tb), lambda i: (0, 0, i))
    return pl.pallas_call(
        functools.partial(_route_kernel, tb=tb),
        grid=(n // tb,),
        in_specs=[pl.BlockSpec((tb, d), lambda i: (i, 0)),
                  pl.BlockSpec(wq_t.shape, lambda i: (0, 0)),
                  pl.BlockSpec(kbd.shape, lambda i: (0, 0, 0))],
        out_specs=[kspec, kspec, hspec, hspec],
        out_shape=[by_key, by_key, by_head, by_head],
        scratch_shapes=[pltpu.VMEM((2 * nh * nk, tb), BF16),
                        pltpu.VMEM((2, nk * nh, tb), F32),
                        pltpu.VMEM((nk * nh, LANES), F32),
                        pltpu.VMEM((2, nk * nh, LANES), F32),
                        pltpu.VMEM((2, PEER_TOPK * nh, LANES), F32),
                        pltpu.VMEM((nk * nh, LANES), F32)],
        compiler_params=pltpu.CompilerParams(dimension_semantics=("parallel",),
                                             vmem_limit_bytes=VMEM_LIMIT),
        name="peer_route",
    )(x1, wq_t, kbd)


def _route_weights(peer_wq_l, peer_keys_l):
    nh, nk, c = PEER_HEADS, PEER_N_KEYS, PEER_QDIM // 2
    d = peer_wq_l.shape[0]
    wq_t = peer_wq_l.T.reshape(nh, 2, c, d).transpose(1, 0, 2, 3).reshape(2 * nh * c, d)
    kbd = jnp.einsum('hpkc,hg->pkhgc', peer_keys_l, jnp.eye(nh, dtype=peer_keys_l.dtype))
    return wq_t.astype(BF16), kbd.reshape(2, nk * nh, nh * c).astype(BF16)


def _dense_kernel(xa_ref, xb_ref, am_ref, cnt_ref, e1_ref, rk_ref, u_ref, vt_ref, lg_ref, lb_ref,
                  o_ref, xt_scr, h0_scr, h1_scr, a0_scr, a1_scr, acc_scr, *, tb, ec, nch, alpha):
    s = pl.program_id(0)
    nk = PEER_N_KEYS
    phase = s % nch
    h_bufs = (h0_scr, h1_scr)
    a_bufs = (a0_scr, a1_scr)

    @pl.when(s == 0)
    def _():
        for buf in h_bufs + a_bufs + (acc_scr,):
            buf[...] = jnp.zeros_like(buf)

    @pl.when(phase == 0)
    def _():
        xt_scr[...] = xa_ref[...].T.astype(BF16)

    @pl.when(phase == 2 % nch)
    def _():
        acc_scr[...] = jnp.zeros_like(acc_scr)

    def stage(cur, part):
        h_cur, h_prv = h_bufs[cur], h_bufs[1 - cur]
        a_cur, a_prv = a_bufs[cur], a_bufs[1 - cur]
        tw = tb // DENSE_PARTS
        toks = slice(part * tw, (part + 1) * tw)
        acc_scr[:, toks] += _dot(vt_ref[...], a_cur[:, toks])
        h_cur[:, toks] = _dot(u_ref[...], xt_scr[:, toks])
        packed = (nk // BF16_ROWS, BF16_ROWS, LANES)
        zero = jnp.zeros((), BF16)
        for i0 in range(0, ec // nk, GATE_KEYS):
            keys = range(i0, i0 + GATE_KEYS)
            for lb in range(part * tw // LANES, (part + 1) * tw // LANES):
                lanes = slice(lb * LANES, (lb + 1) * LANES)
                gates = [jnp.zeros(packed, BF16) for _ in keys]
                for h in range(PEER_HEADS):
                    rk = pltpu.bitcast(rk_ref[h, :, lanes], BF16).reshape(packed)
                    e1 = pltpu.bitcast(e1_ref[h, :, lanes], BF16).reshape(packed)
                    for g, ii in enumerate(keys):
                        am = jnp.broadcast_to(am_ref[ii, h:h + 1, lanes], packed[1:]).astype(BF16)
                        cnt = jnp.broadcast_to(cnt_ref[ii, h:h + 1, lanes], packed[1:]).astype(BF16)
                        gates[g] = gates[g] + jnp.where(cnt[None] - rk > 0, am[None] * e1, zero)
                for g, ii in enumerate(keys):
                    rows = slice(ii * nk, (ii + 1) * nk)
                    hv = h_prv[rows, lanes]
                    half = 0.5 * hv
                    gelu = half + half * lax.erf(hv * (0.5 ** 0.5))
                    a_prv[rows, lanes] = (gelu.astype(BF16).reshape(packed) * gates[g]).reshape(nk, LANES)

    for par in range(2):
        for part in range(DENSE_PARTS):
            pl.when((s + 2 * part) % 2 == par)(functools.partial(stage, par, part))

    @pl.when((phase == 1 % nch) & (s > 1))
    def _():
        y = alpha * xb_ref[...] + acc_scr[...].T
        o_ref[...] = _layer_norm(y, lg_ref[...], lb_ref[...])


def _dense(x1, am, cnt, e1, rk, u, vt, lg, lb, tb, ec, alpha):
    n, d = x1.shape
    n_exp = u.shape[0]
    nk = PEER_N_KEYS
    nch = n_exp // ec
    assert nch % 2 == 0 and nch >= 4
    total = (n // tb) * nch

    def m1(s):
        return jnp.minimum(s, total - 1)

    def gt(s):
        return jnp.clip(s - 1, 0, total - 1)

    def m2(s):
        return jnp.clip(s - 2, 0, total - 1)

    rspec = pl.BlockSpec((PEER_HEADS, nk // 2, tb), lambda s: (0, 0, gt(s) // nch))
    cspec = pl.BlockSpec((ec // nk, PEER_HEADS, tb), lambda s: (gt(s) % nch, 0, gt(s) // nch))
    return pl.pallas_call(
        functools.partial(_dense_kernel, tb=tb, ec=ec, nch=nch, alpha=alpha),
        grid=(total + 2,),
        in_specs=[pl.BlockSpec((tb, d), lambda s: (m1(s) // nch, 0)),
                  pl.BlockSpec((tb, d), lambda s: (m2(s) // nch, 0)),
                  cspec, cspec, rspec, rspec,
                  pl.BlockSpec((ec, d), lambda s: (m1(s) % nch, 0)),
                  pl.BlockSpec((d, ec), lambda s: (0, m2(s) % nch)),
                  pl.BlockSpec(lg.shape, lambda s: (0, 0)),
                  pl.BlockSpec(lb.shape, lambda s: (0, 0))],
        out_specs=pl.BlockSpec((tb, d), lambda s: (m2(s) // nch, 0)),
        out_shape=jax.ShapeDtypeStruct((n, d), F32),
        scratch_shapes=[pltpu.VMEM((d, tb), BF16),
                        pltpu.VMEM((ec, tb), F32), pltpu.VMEM((ec, tb), F32),
                        pltpu.VMEM((ec, tb), BF16), pltpu.VMEM((ec, tb), BF16),
                        pltpu.VMEM((d, tb), F32)],
        compiler_params=pltpu.CompilerParams(dimension_semantics=("arbitrary",),
                                             vmem_limit_bytes=VMEM_LIMIT),
        name="peer_dense",
    )(x1, x1, am, cnt, e1, rk, u, vt, lg, lb)


def kernel(x, w_in, w_o, attn_sink, na_rpb, t5_table, gnorm_a, gnorm_b, ln1_g, ln1_b, ln2_g, ln2_b,
           peer_wq, peer_keys, peer_u, peer_v):
    b, seq, d = x.shape
    depth = w_in.shape[0]
    n = b * seq
    alpha = (2 * depth) ** 0.25
    tq = min(512, seq)
    tm = min(512, n)
    tb_route = min(512, n)
    tb_dense = min(512, n)
    ec = 2048
    assert seq % tq == 0 and seq // GRID_W >= NA_KEY_ROWS and seq // BLOCK >= 3

    wb = _window_bias_table(t5_table)
    cb_all = _na_bias_table(na_rpb)
    row = lambda v: v.reshape(1, -1)

    for l in range(depth):
        hq, hkv = _proj(x.reshape(n, d), _proj_weight(w_in[l]), tm)
        x1 = _attn(attn_sink[l], hq.reshape(b, seq, Q_WIDTH), hkv.reshape(b, seq, KV_WIDTH), x,
                   wb, cb_all[l], w_o[l].astype(BF16), row(gnorm_a[l]), row(gnorm_b[l]),
                   row(ln1_g[l]), row(ln1_b[l]), tq, alpha)
        x1 = x1.reshape(n, d)
        wq_t, kbd = _route_weights(peer_wq[l], peer_keys[l])
        am, cnt, e1, rk = _route(x1, wq_t, kbd, tb_route)
        x = _dense(x1, am, cnt, e1, rk, peer_u[l].astype(BF16), peer_v[l].T.astype(BF16),
                   row(ln2_g[l]), row(ln2_b[l]), tb_dense, ec, alpha).reshape(b, seq, d)
    return x
```
